```python
import math
import jax, jax.numpy as jnp
from jax import lax
import numpy as np

D_MODEL = 1024
BATCH = 16
SEQ = 2048
DEPTH = 2

CTX_LEN = 256
GRID_W = 64
ROPE_BASE = 10000.0
Q_BLOCK = 128
EPS = 1e-6

MLA_HEADS = 8
MLA_NOPE = 64
MLA_ROPE = 32
MLA_V = 64
MLA_Q_RANK = 384
MLA_KV_RANK = 256
MLA_WIDTH = MLA_HEADS * MLA_V

POOL_WINDOWS = (2, 4, 8, 16)
POOL_GROUPS = 4
POOL_GROUP_W = 128
POOL_WIDTH = POOL_GROUPS * POOL_GROUP_W

DIFF_HEADS = 4
DIFF_HALF = 64
DIFF_V = 128
DIFF_QK = DIFF_HEADS * 2 * DIFF_HALF
DIFF_WIDTH = DIFF_HEADS * DIFF_V

N_BRANCH = 3

N_EXPERTS = 64
TOP_K = 8
N_GROUPS = 8
TOPK_GROUPS = 4
EXPERT_FF = 256
SHARED_FF = 256
ROUTED_SCALE = 2.5
EXPERT_BLOCK = 256

KV_SPLIT = (MLA_KV_RANK, MLA_KV_RANK + MLA_ROPE, MLA_KV_RANK + MLA_ROPE + DIFF_QK)
KV_COLS = MLA_KV_RANK + MLA_ROPE + DIFF_QK + DIFF_WIDTH
Q_SPLIT = (MLA_Q_RANK, MLA_Q_RANK + DIFF_QK, MLA_Q_RANK + DIFF_QK + POOL_WIDTH)
Q_COLS = MLA_Q_RANK + DIFF_QK + POOL_WIDTH + N_BRANCH * D_MODEL
IN_COLS = KV_COLS + Q_COLS

kernel_name = 'hybrid_mla_pool_diffattn_moe_dit'


def rmsnorm(x, g):
    xf = x.astype(jnp.float32)
    y = xf * lax.rsqrt(jnp.mean(xf * xf, axis=-1, keepdims=True) + EPS)
    return (y * g.astype(jnp.float32)).astype(x.dtype)


def modulate(h, shift, scale):
    return h * (1.0 + scale) + shift


def axial_angles(n_tok, rot_dim):
    rows = n_tok // GRID_W
    pos_row = jnp.repeat(jnp.arange(rows, dtype=jnp.float32), GRID_W)
    pos_col = jnp.tile(jnp.arange(GRID_W, dtype=jnp.float32), rows)
    d_axis = rot_dim // 2
    inv_freq = ROPE_BASE ** (-jnp.arange(0, d_axis, 2, dtype=jnp.float32) / d_axis)
    return pos_row[:, None] * inv_freq, pos_col[:, None] * inv_freq


def rope_axis(x, ang):
    ang = ang.reshape(ang.shape[0], *([1] * (x.ndim - 3)), ang.shape[-1])
    cos = jnp.cos(ang).astype(x.dtype)
    sin = jnp.sin(ang).astype(x.dtype)
    x1, x2 = jnp.split(x, 2, axis=-1)
    return jnp.concatenate([x1 * cos - x2 * sin, x1 * sin + x2 * cos], axis=-1)


def rope_2d(x, ang_row, ang_col):
    xr, xc = jnp.split(x, 2, axis=-1)
    return jnp.concatenate([rope_axis(xr, ang_row), rope_axis(xc, ang_col)], axis=-1)


def kv_streams(p_kv, g_ckv, w_ukv, ang_mla, ang_diff):
    b, n = p_kv.shape[:2]
    c_kv, k_rope, k_diff, v_diff = jnp.split(p_kv, KV_SPLIT, axis=-1)
    kv = (rmsnorm(c_kv, g_ckv) @ w_ukv).reshape(b, n, MLA_HEADS, MLA_NOPE + MLA_V)
    k_nope, v_mla = kv[..., :MLA_NOPE], kv[..., MLA_NOPE:]
    k_diff = k_diff.reshape(b, n, DIFF_HEADS, 2, DIFF_HALF)
    v_diff = v_diff.reshape(b, n, DIFF_HEADS, DIFF_V)
    if ang_mla is not None:
        k_rope = rope_2d(k_rope, *ang_mla)
        k_diff = rope_2d(k_diff, *ang_diff)
    return k_nope, k_rope, v_mla, k_diff, v_diff


def q_streams(p_q, g_cq, w_uq, ang_mla, ang_diff):
    b, n = p_q.shape[:2]
    c_q, q_diff, pool_in, gate_logits = jnp.split(p_q, Q_SPLIT, axis=-1)
    q = (rmsnorm(c_q, g_cq) @ w_uq).reshape(b, n, MLA_HEADS, MLA_NOPE + MLA_ROPE)
    q_nope, q_rope = q[..., :MLA_NOPE], q[..., MLA_NOPE:]
    q_diff = q_diff.reshape(b, n, DIFF_HEADS, 2, DIFF_HALF)
    if ang_mla is not None:
        q_rope = rope_2d(q_rope, *ang_mla)
        q_diff = rope_2d(q_diff, *ang_diff)
    return q_nope, q_rope, q_diff, pool_in, gate_logits


def mla_attend(k_nope, k_rope, v):
    scale = 1.0 / math.sqrt(MLA_NOPE + MLA_ROPE)

    def attend(qs):
        q_nope, q_rope = qs
        s = (jnp.einsum('bqhd,bkhd->bhqk', q_nope, k_nope)
             + jnp.einsum('bqhd,bkd->bhqk', q_rope, k_rope))
        p = jax.nn.softmax(s.astype(jnp.float32) * scale, axis=-1).astype(v.dtype)
        return jnp.einsum('bhqk,bkhd->bqhd', p, v)
    return attend


def diff_attend(k, v, lam):
    scale = 1.0 / math.sqrt(DIFF_HALF)

    def attend(qs):
        (q,) = qs
        s = jnp.einsum('bqhmd,bkhmd->bmhqk', q, k).astype(jnp.float32) * scale
        p = jax.nn.softmax(s, axis=-1)
        a = (p[:, 0] - lam * p[:, 1]).astype(v.dtype)
        return jnp.einsum('bhqk,bkhd->bqhd', a, v)
    return attend


def sweep_queries(attend, qs):
    b, n = qs[0].shape[:2]
    nb = n // Q_BLOCK
    blocks = tuple(jnp.moveaxis(q.reshape(b, nb, Q_BLOCK, *q.shape[2:]), 1, 0) for q in qs)
    o = lax.map(attend, blocks)
    return jnp.moveaxis(o, 0, 1).reshape(b, n, *o.shape[3:])


def pool_mix(u, w_pool, pool_scale):
    b, n, _ = u.shape
    ug = u.reshape(b, n, POOL_GROUPS, POOL_GROUP_W)
    t = jnp.arange(n)
    outs = []
    for g, w in enumerate(POOL_WINDOWS):
        xg = ug[:, :, g].astype(jnp.float32)
        cs = jnp.concatenate([jnp.zeros((b, 1, POOL_GROUP_W), jnp.float32), lax.cumsum(xg, axis=1)], axis=1)
        lo = jnp.clip(t - w // 2, 0, n)
        hi = jnp.clip(t - w // 2 + w, 0, n)
        mean = (cs[:, hi] - cs[:, lo]) / (hi - lo).astype(jnp.float32)[None, :, None]
        outs.append((mean - xg).astype(u.dtype))
    pooled = jnp.stack(outs, axis=2)
    mixed = jnp.einsum('blgc,gcd->blgd', pooled, w_pool)
    return mixed.reshape(b, n, POOL_WIDTH) * pool_scale


def merge_branches(o_mla, o_diff, pool_in, gate_logits, lam_init, w_pool, pool_scale, g_subln,
                   w_b_mla, w_b_pool, w_b_diff, w_out):
    b, n = pool_in.shape[:2]
    o_mla = o_mla.reshape(b, n, MLA_WIDTH)
    o_diff = (rmsnorm(o_diff, g_subln) * (1.0 - lam_init)).reshape(b, n, DIFF_WIDTH)
    o_pool = pool_mix(pool_in, w_pool, pool_scale)
    gates = jax.nn.sigmoid(gate_logits.astype(jnp.float32)).astype(pool_in.dtype)
    gates = gates.reshape(b, n, N_BRANCH, D_MODEL)
    merged = (gates[:, :, 0] * (o_mla @ w_b_mla)
              + gates[:, :, 1] * (o_pool @ w_b_pool)
              + gates[:, :, 2] * (o_diff @ w_b_diff))
    return merged @ w_out


def token_mixing(hx, hc, ctx_out, lam_init, w_in, g_cq, w_uq, g_ckv, w_ukv, w_pool, pool_scale,
                 lam_q1, lam_k1, lam_q2, lam_k2, g_subln, w_b_mla, w_b_pool, w_b_diff, w_out):
    n_lat = hx.shape[1]
    ang_mla = axial_angles(n_lat, MLA_ROPE)
    ang_diff = axial_angles(n_lat, DIFF_HALF)
    px = hx @ w_in
    pc = hc @ w_in if ctx_out else hc @ w_in[:, :KV_COLS]
    kv_x = kv_streams(px[..., :KV_COLS], g_ckv, w_ukv, ang_mla, ang_diff)
    kv_c = kv_streams(pc[..., :KV_COLS], g_ckv, w_ukv, None, None)
    k_nope, k_rope, v_mla, k_diff, v_diff = [jnp.concatenate([kc_, kx_], axis=1) for kc_, kx_ in zip(kv_c, kv_x)]
    lam = (jnp.exp(jnp.sum((lam_q1 * lam_k1).astype(jnp.float32)))
           - jnp.exp(jnp.sum((lam_q2 * lam_k2).astype(jnp.float32))) + lam_init)
    q_nope, q_rope, q_diff, pool_in, gate_logits = q_streams(px[..., KV_COLS:], g_cq, w_uq, ang_mla, ang_diff)
    o_mla = sweep_queries(mla_attend(k_nope, k_rope, v_mla), (q_nope, q_rope))
    o_diff = sweep_queries(diff_attend(k_diff, v_diff, lam), (q_diff,))
    y_x = merge_branches(o_mla, o_diff, pool_in, gate_logits, lam_init, w_pool, pool_scale, g_subln,
                         w_b_mla, w_b_pool, w_b_diff, w_out)
    if not ctx_out:
        return y_x, None
    cq_nope, cq_rope, cq_diff, c_pool_in, c_gate_logits = q_streams(pc[..., KV_COLS:], g_cq, w_uq, None, None)
    oc_mla = mla_attend(kv_c[0], kv_c[1], kv_c[2])((cq_nope, cq_rope))
    oc_diff = diff_attend(kv_c[3], kv_c[4], lam)((cq_diff,))
    y_c = merge_branches(oc_mla, oc_diff, c_pool_in, c_gate_logits, lam_init, w_pool, pool_scale, g_subln,
                         w_b_mla, w_b_pool, w_b_diff, w_out)
    return y_x, y_c


def moe_ffn(h, w_router, b_router, w_e_gate, w_e_up, w_e_down, w_s_gate, w_s_up, w_s_down):
    n = h.shape[0]
    scores = jax.nn.sigmoid((h @ w_router).astype(jnp.float32))
    sel = scores + b_router.astype(jnp.float32)
    grp = sel.reshape(n, N_GROUPS, N_EXPERTS // N_GROUPS)
    grp_score = lax.top_k(grp, 2)[0].sum(-1)
    _, top_grp = lax.top_k(grp_score, TOPK_GROUPS)
    grp_mask = jnp.any(top_grp[:, :, None] == jnp.arange(N_GROUPS)[None, None, :], axis=1)
    exp_mask = jnp.repeat(grp_mask, N_EXPERTS // N_GROUPS, axis=1)
    _, idx = lax.top_k(jnp.where(exp_mask, sel, -jnp.inf), TOP_K)
    wts = jnp.take_along_axis(scores, idx, axis=1)
    wts = wts / jnp.sum(wts, axis=-1, keepdims=True) * ROUTED_SCALE
    nk = n * TOP_K
    flat_e = idx.reshape(-1)
    order = jnp.argsort(flat_e)
    sorted_e = flat_e[order]
    counts = jnp.bincount(flat_e, length=N_EXPERTS)
    padded = (counts + EXPERT_BLOCK - 1) // EXPERT_BLOCK * EXPERT_BLOCK
    pad_end = jnp.cumsum(padded)
    pad_start = pad_end - padded
    raw_start = jnp.cumsum(counts) - counts
    dest = pad_start[sorted_e] + jnp.arange(nk) - raw_start[sorted_e]
    n_rows = (-(-nk // EXPERT_BLOCK)) * EXPERT_BLOCK + N_EXPERTS * EXPERT_BLOCK
    n_blk = n_rows // EXPERT_BLOCK
    row_tok = jnp.zeros((n_rows,), jnp.int32).at[dest].set((order // TOP_K).astype(jnp.int32))
    row_w = jnp.zeros((n_rows,), jnp.float32).at[dest].set(wts.reshape(-1)[order])
    blk_e = jnp.minimum(jnp.searchsorted(pad_end, jnp.arange(n_blk) * EXPERT_BLOCK, side='right'), N_EXPERTS - 1)

    def expert_block(acc, blk):
        tok, w, e = blk
        xb = h[tok]
        y = (jax.nn.silu(xb @ w_e_gate[e]) * (xb @ w_e_up[e])) @ w_e_down[e]
        return acc.at[tok].add(y * w[:, None].astype(y.dtype)), None

    routed, _ = lax.scan(expert_block, jnp.zeros_like(h),
                         (row_tok.reshape(n_blk, EXPERT_BLOCK), row_w.reshape(n_blk, EXPERT_BLOCK), blk_e))
    shared = (jax.nn.silu(h @ w_s_gate) * (h @ w_s_up)) @ w_s_down
    return routed + shared


def setup_inputs(seed: int = 0) -> dict:
    key = jax.random.key(seed)
    ks = iter(jax.random.split(key, 64))

    def nrm(shape, scale):
        return jax.random.normal(next(ks), shape, jnp.float32) * scale

    def gain(shape):
        return 1.0 + nrm(shape, 0.02)

    L, D = DEPTH, D_MODEL
    return {
        'x': nrm((BATCH, SEQ, D), 1.0),
        'c': nrm((BATCH, D), 1.0),
        'ctx': nrm((BATCH, CTX_LEN, D), 1.0),
        'c_ctx': nrm((D,), 1.0),
        'w_mod': nrm((L, D, 6 * D), 0.5 * D ** -0.5),
        'b_mod': nrm((L, 6 * D), 0.02),
        'g_pre_mix': gain((L, D)),
        'g_post_mix': gain((L, D)),
        'g_pre_ffn': gain((L, D)),
        'g_post_ffn': gain((L, D)),
        'w_in': nrm((L, D, IN_COLS), D ** -0.5),
        'g_cq': gain((L, MLA_Q_RANK)),
        'w_uq': nrm((L, MLA_Q_RANK, MLA_HEADS * (MLA_NOPE + MLA_ROPE)), MLA_Q_RANK ** -0.5),
        'g_ckv': gain((L, MLA_KV_RANK)),
        'w_ukv': nrm((L, MLA_KV_RANK, MLA_HEADS * (MLA_NOPE + MLA_V)), MLA_KV_RANK ** -0.5),
        'w_pool': nrm((L, POOL_GROUPS, POOL_GROUP_W, POOL_GROUP_W), POOL_GROUP_W ** -0.5),
        'pool_scale': 1.0 + nrm((L, POOL_WIDTH), 0.1),
        'lam_q1': nrm((L, DIFF_HALF), 0.1),
        'lam_k1': nrm((L, DIFF_HALF), 0.1),
        'lam_q2': nrm((L, DIFF_HALF), 0.1),
        'lam_k2': nrm((L, DIFF_HALF), 0.1),
        'g_subln': gain((L, DIFF_V)),
        'w_b_mla': nrm((L, MLA_WIDTH, D), MLA_WIDTH ** -0.5),
        'w_b_pool': nrm((L, POOL_WIDTH, D), POOL_WIDTH ** -0.5),
        'w_b_diff': nrm((L, DIFF_WIDTH, D), DIFF_WIDTH ** -0.5),
        'w_out': nrm((L, D, D), D ** -0.5),
        'w_router': nrm((L, D, N_EXPERTS), D ** -0.5),
        'b_router': nrm((L, N_EXPERTS), 0.01),
        'w_e_gate': nrm((L, N_EXPERTS, D, EXPERT_FF), D ** -0.5),
        'w_e_up': nrm((L, N_EXPERTS, D, EXPERT_FF), D ** -0.5),
        'w_e_down': nrm((L, N_EXPERTS, EXPERT_FF, D), EXPERT_FF ** -0.5),
        'w_s_gate': nrm((L, D, SHARED_FF), D ** -0.5),
        'w_s_up': nrm((L, D, SHARED_FF), D ** -0.5),
        'w_s_down': nrm((L, SHARED_FF, D), SHARED_FF ** -0.5),
    }


def reference(x, c, ctx, c_ctx, w_mod, b_mod, g_pre_mix, g_post_mix, g_pre_ffn, g_post_ffn, w_in,
              g_cq, w_uq, g_ckv, w_ukv, w_pool, pool_scale, lam_q1, lam_k1, lam_q2, lam_k2, g_subln,
              w_b_mla, w_b_pool, w_b_diff, w_out, w_router, b_router, w_e_gate, w_e_up, w_e_down,
              w_s_gate, w_s_up, w_s_down):
    b, n_lat, d = x.shape
    h_lat, h_ctx = x, ctx
    for l in range(DEPTH):
        ctx_out = l < DEPTH - 1
        lam_init = 0.8 - 0.6 * math.exp(-0.3 * l)
        m_x = [m[:, None, :] for m in jnp.split(jax.nn.silu(c) @ w_mod[l] + b_mod[l], 6, axis=-1)]
        m_c = jnp.split(jax.nn.silu(c_ctx) @ w_mod[l] + b_mod[l], 6, axis=-1)
        hx = modulate(rmsnorm(h_lat, g_pre_mix[l]), m_x[0], m_x[1])
        hc = modulate(rmsnorm(h_ctx, g_pre_mix[l]), m_c[0], m_c[1])
        y_x, y_c = token_mixing(hx, hc, ctx_out, lam_init, w_in[l], g_cq[l], w_uq[l], g_ckv[l], w_ukv[l],
                                w_pool[l], pool_scale[l], lam_q1[l], lam_k1[l], lam_q2[l], lam_k2[l],
                                g_subln[l], w_b_mla[l], w_b_pool[l], w_b_diff[l], w_out[l])
        h_lat = h_lat + m_x[2] * rmsnorm(y_x, g_post_mix[l])
        fx = modulate(rmsnorm(h_lat, g_pre_ffn[l]), m_x[3], m_x[4]).reshape(-1, d)
        moe_args = (w_router[l], b_router[l], w_e_gate[l], w_e_up[l], w_e_down[l],
                    w_s_gate[l], w_s_up[l], w_s_down[l])
        if ctx_out:
            h_ctx = h_ctx + m_c[2] * rmsnorm(y_c, g_post_mix[l])
            fc = modulate(rmsnorm(h_ctx, g_pre_ffn[l]), m_c[3], m_c[4]).reshape(-1, d)
            f_all = moe_ffn(jnp.concatenate([fx, fc], axis=0), *moe_args)
            f_x, f_c = f_all[: fx.shape[0]], f_all[fx.shape[0]:]
            h_ctx = h_ctx + m_c[5] * rmsnorm(f_c.reshape(h_ctx.shape), g_post_ffn[l])
        else:
            f_x = moe_ffn(fx, *moe_args)
        h_lat = h_lat + m_x[5] * rmsnorm(f_x.reshape(b, n_lat, d), g_post_ffn[l])
    return h_lat
```

```python
import functools
import math

import jax
import jax.numpy as jnp
from jax import lax
from jax.experimental import pallas as pl
from jax.experimental.pallas import tpu as pltpu

F32 = jnp.float32
BF16 = jnp.bfloat16
I32 = jnp.int32

EPS = 1e-6
ROPE_BASE = 10000.0
GRID_W = 64

MLA_HEADS = 8
MLA_NOPE = 64
MLA_ROPE = 32
MLA_V = 64
MLA_Q_RANK = 384
MLA_KV_RANK = 256
POOL_WINDOWS = (2, 4, 8, 16)
POOL_GROUP_W = 128
DIFF_HEADS = 4
DIFF_HALF = 64
DIFF_V = 128
N_EXPERTS = 64
TOP_K = 8
N_GROUPS = 8
TOPK_GROUPS = 4
EXPERT_FF = 256
ROUTED_SCALE = 2.5

LANES = 128
TM = 256
ALIGN = 16
EBLK = 256
RL = TM * TOP_K + N_EXPERTS * ALIGN
RCH = 512
VMEM_LIMIT = 56 * 1024 * 1024

C_CKV = 0
C_KR = C_CKV + MLA_KV_RANK
C_KD = C_KR + LANES
C_VD = C_KD + 512
C_CQ = C_VD + 512
C_QD = C_CQ + MLA_Q_RANK
C_POOL = C_QD + 512
C_GATE = C_POOL + 512


def _rms(x, g):
    ms = jnp.mean(x * x, axis=-1, keepdims=True)
    return x * lax.rsqrt(ms + EPS) * g


def _dot(a, b):
    return jnp.dot(a, b, preferred_element_type=F32)


def _dot_nt(a, b):
    return lax.dot_general(a, b, (((1,), (1,)), ((), ())), preferred_element_type=F32)


def _split_bf16(x):
    hi = x.astype(BF16)
    lo = (x - hi.astype(F32)).astype(BF16)
    return hi, lo


def _silu(x):
    return x * jax.nn.sigmoid(x)


def _rope(x, tab, quarter):
    w = x.shape[-1]
    rep = w // LANES
    c, s1, s2 = (jnp.tile(t, (1, rep)) if rep > 1 else t for t in tab)
    return x * c + pltpu.roll(x, w - quarter, 1) * s1 + pltpu.roll(x, quarter, 1) * s2


def _mod_kernel(a_ref, w_ref, b_ref, o_ref):
    a = _silu(a_ref[...])
    ahi, alo = _split_bf16(a)
    whi, wlo = _split_bf16(w_ref[...])
    o_ref[...] = _dot(ahi, whi) + _dot(alo, whi) + _dot(ahi, wlo) + b_ref[...]


def _modulation(cvec, w_mod, b_mod):
    nl, d, d6 = w_mod.shape
    rows = cvec.shape[0]
    nc = d6 // d
    return pl.pallas_call(
        _mod_kernel,
        grid=(nl, nc),
        in_specs=[
            pl.BlockSpec((rows, d), lambda l, j: (0, 0)),
            pl.BlockSpec((None, d, d), lambda l, j: (l, 0, j)),
            pl.BlockSpec((None, 1, d), lambda l, j: (l, 0, j)),
        ],
        out_specs=pl.BlockSpec((None, rows, d), lambda l, j: (l, 0, j)),
        out_shape=jax.ShapeDtypeStruct((nl, rows, d6), F32),
        name="modulation",
    )(cvec, w_mod, b_mod.reshape(nl, 1, d6))


def _proj_kernel(h_ref, mod_ref, gpre_ref, w1_ref, gckv_ref, wukv_ref, gcq_ref, wuq_ref, tm_ref, td_ref,
                 qm_ref, km_ref, vm_ref, qd_ref, kd_ref, vd_ref, pool_ref, gate_ref):
    d = h_ref.shape[-1]
    m = mod_ref[0, 0]
    hx = (_rms(h_ref[0], gpre_ref[...]) * (1.0 + m[1:2]) + m[0:1]).astype(BF16)
    tabm = (tm_ref[0], tm_ref[1], tm_ref[2])
    tabd = (td_ref[0], td_ref[1], td_ref[2])

    ckv = _dot(hx, w1_ref[:, C_CKV:C_KR])
    kv = _dot(_rms(ckv, gckv_ref[...]).astype(BF16), wukv_ref[...])
    kr = _rope(_dot(hx, w1_ref[:, C_KR:C_KD]), tabm, MLA_ROPE // 4)
    nk = MLA_HEADS * LANES
    km_ref[0] = (kv[:, :nk] + jnp.tile(kr, (1, MLA_HEADS))).astype(BF16)
    vm_ref[0] = kv[:, nk:].astype(BF16)

    kd_ref[0] = _rope(_dot(hx, w1_ref[:, C_KD:C_VD]), tabd, DIFF_HALF // 4).astype(BF16)
    vd_ref[0] = _dot(hx, w1_ref[:, C_VD:C_CQ]).astype(BF16)

    cq = _dot(hx, w1_ref[:, C_CQ:C_QD])
    q = _dot(_rms(cq, gcq_ref[...]).astype(BF16), wuq_ref[...])
    qm_ref[0] = _rope(q, tabm, MLA_ROPE // 4).astype(BF16)
    qd_ref[0] = _rope(_dot(hx, w1_ref[:, C_QD:C_POOL]), tabd, DIFF_HALF // 4).astype(BF16)
    pool_ref[0] = _dot(hx, w1_ref[:, C_POOL:C_GATE]).astype(BF16)
    for j in range(3):
        lo = C_GATE + j * d
        gate_ref[0, :, j * d:(j + 1) * d] = jax.nn.sigmoid(_dot(hx, w1_ref[:, lo:lo + d])).astype(BF16)


def _const_spec(shape):
    nd = len(shape)
    return pl.BlockSpec(shape, lambda *_: (0,) * nd, pipeline_mode=pl.Buffered(1))


def _project(h, mods, g_pre, w1, g_ckv, w_ukv, g_cq, w_uq, tab_m, tab_d):
    b, lc, d = h.shape
    nt = lc // TM
    tile = lambda w: pl.BlockSpec((1, TM, w), lambda bi, i: (bi, i, 0))
    widths = (MLA_HEADS * LANES, MLA_HEADS * LANES, MLA_HEADS * MLA_V, 512, 512, 512, 512, 3 * d)
    return pl.pallas_call(
        _proj_kernel,
        grid=(b, nt),
        in_specs=[
            tile(d),
            pl.BlockSpec((1, 1, 6, d), lambda bi, i: (bi, jnp.minimum(i, 1), 0, 0)),
            _const_spec(g_pre.shape),
            _const_spec(w1.shape),
            _const_spec(g_ckv.shape),
            _const_spec(w_ukv.shape),
            _const_spec(g_cq.shape),
            _const_spec(w_uq.shape),
            pl.BlockSpec((3, TM, LANES), lambda bi, i: (0, i, 0)),
            pl.BlockSpec((3, TM, LANES), lambda bi, i: (0, i, 0)),
        ],
        out_specs=[tile(w) for w in widths],
        out_shape=[jax.ShapeDtypeStruct((b, lc, w), BF16) for w in widths],
        compiler_params=pltpu.CompilerParams(
            dimension_semantics=("parallel", "parallel"), vmem_limit_bytes=VMEM_LIMIT),
        name="project",
    )(h, mods, g_pre, w1, g_ckv, w_ukv, g_cq, w_uq, tab_m, tab_d)


def _exp_parts(s, scale):
    m = jnp.max(s, axis=-1, keepdims=True)
    e = jnp.exp((s - m) * scale)
    return e, jnp.sum(e, axis=-1, keepdims=True)


def _mla_kernel(q_ref, k_ref, v_ref, o_ref, *, q_off, n_ctx):
    scale = 1.0 / math.sqrt(MLA_NOPE + MLA_ROPE)
    lane = lax.broadcasted_iota(I32, (1, LANES), 1)

    def attend(nkeys):
        outs = []
        for j in range(2):
            q = q_ref[0, :, j * LANES:(j + 1) * LANES]
            k = k_ref[0, :nkeys, j * LANES:(j + 1) * LANES]
            e, l = _exp_parts(_dot_nt(q, k), scale)
            outs.append(_dot(e.astype(BF16), v_ref[0, :nkeys, :]) * (1.0 / l))
        o_ref[0] = jnp.where(lane < MLA_V, outs[0], outs[1]).astype(o_ref.dtype)

    if q_off == 0:
        qi = pl.program_id(2)
        pl.when(qi == 0)(lambda: attend(n_ctx))
        pl.when(qi != 0)(lambda: attend(k_ref.shape[1]))
    else:
        attend(k_ref.shape[1])


def _mla_attention(qm, km, vm, q_off, n_ctx):
    b, lc, _ = qm.shape
    nq = lc // TM - q_off
    return pl.pallas_call(
        functools.partial(_mla_kernel, q_off=q_off, n_ctx=n_ctx),
        grid=(b, MLA_HEADS // 2, nq),
        in_specs=[
            pl.BlockSpec((1, TM, 2 * LANES), lambda bi, hp, i: (bi, i + q_off, hp)),
            pl.BlockSpec((1, lc, 2 * LANES), lambda bi, hp, i: (bi, 0, hp)),
            pl.BlockSpec((1, lc, 2 * MLA_V), lambda bi, hp, i: (bi, 0, hp)),
        ],
        out_specs=pl.BlockSpec((1, TM, 2 * MLA_V), lambda bi, hp, i: (bi, i, hp)),
        out_shape=jax.ShapeDtypeStruct((b, nq * TM, MLA_HEADS * MLA_V), BF16),
        compiler_params=pltpu.CompilerParams(
            dimension_semantics=("parallel", "parallel", "arbitrary"), vmem_limit_bytes=VMEM_LIMIT),
        name="mla_attention",
    )(qm, km, vm)


def _diff_kernel(q_ref, k_ref, v_ref, lam_ref, g_ref, o_ref, *, q_off, n_ctx, lam_init):
    scale = 1.0 / math.sqrt(DIFF_HALF)
    lane = lax.broadcasted_iota(I32, (1, LANES), 1)
    lp = lam_ref[...]
    lam = (jnp.exp(jnp.sum(lp[0:1] * lp[1:2], axis=-1, keepdims=True))
           - jnp.exp(jnp.sum(lp[2:3] * lp[3:4], axis=-1, keepdims=True)) + lam_init)

    def attend(nkeys):
        q = q_ref[0]
        k = k_ref[0, :nkeys, :]
        v = v_ref[0, :nkeys, :]
        zero = jnp.zeros_like(q)
        e0, l0 = _exp_parts(_dot_nt(jnp.where(lane < DIFF_HALF, q, zero), k), scale)
        e1, l1 = _exp_parts(_dot_nt(jnp.where(lane < DIFF_HALF, zero, q), k), scale)
        o = _dot(e0.astype(BF16), v) * (1.0 / l0) - _dot(e1.astype(BF16), v) * (lam / l1)
        o_ref[0] = (_rms(o, g_ref[...]) * (1.0 - lam_init)).astype(o_ref.dtype)

    if q_off == 0:
        qi = pl.program_id(2)
        pl.when(qi == 0)(lambda: attend(n_ctx))
        pl.when(qi != 0)(lambda: attend(k_ref.shape[1]))
    else:
        attend(k_ref.shape[1])


def _diff_attention(qd, kd, vd, lam_p, g_subln, q_off, n_ctx, lam_init):
    b, lc, _ = qd.shape
    nq = lc // TM - q_off
    return pl.pallas_call(
        functools.partial(_diff_kernel, q_off=q_off, n_ctx=n_ctx, lam_init=lam_init),
        grid=(b, DIFF_HEADS, nq),
        in_specs=[
            pl.BlockSpec((1, TM, LANES), lambda bi, h, i: (bi, i + q_off, h)),
            pl.BlockSpec((1, lc, LANES), lambda bi, h, i: (bi, 0, h)),
            pl.BlockSpec((1, lc, DIFF_V), lambda bi, h, i: (bi, 0, h)),
            pl.BlockSpec(lam_p.shape, lambda bi, h, i: (0, 0)),
            pl.BlockSpec(g_subln.shape, lambda bi, h, i: (0, 0)),
        ],
        out_specs=pl.BlockSpec((1, TM, DIFF_V), lambda bi, h, i: (bi, i, h)),
        out_shape=jax.ShapeDtypeStruct((b, nq * TM, DIFF_HEADS * DIFF_V), BF16),
        compiler_params=pltpu.CompilerParams(
            dimension_semantics=("parallel", "parallel", "arbitrary"), vmem_limit_bytes=VMEM_LIMIT),
        name="diff_attention",
    )(qd, kd, vd, lam_p, g_subln)


POOL_HALO = 16


def _pool_kernel(u_ref, w_ref, s_ref, o_ref, pad_ref, *, segments):
    o0 = segments[0][0]
    for s0, n in segments:
        t = lax.broadcasted_iota(I32, (n, 1), 0)
        for g, w in enumerate(POOL_WINDOWS):
            cols = slice(g * POOL_GROUP_W, (g + 1) * POOL_GROUP_W)
            xg = u_ref[0, s0:s0 + n, cols].astype(F32)
            zeros = jnp.zeros((POOL_HALO, POOL_GROUP_W), F32)
            pad_ref[0:POOL_HALO, :] = zeros
            pad_ref[POOL_HALO + n:2 * POOL_HALO + n, :] = zeros
            pad_ref[POOL_HALO:POOL_HALO + n, :] = xg
            acc = pad_ref[POOL_HALO - w // 2:POOL_HALO - w // 2 + n, :]
            for j in range(1 - w // 2, w // 2):
                acc = acc + pad_ref[POOL_HALO + j:POOL_HALO + j + n, :]
            cnt = (jnp.minimum(t + w // 2, n) - jnp.maximum(t - w // 2, 0)).astype(F32)
            pooled = (acc / cnt - xg).astype(BF16)
            o_ref[0, s0 - o0:s0 - o0 + n, cols] = (_dot(pooled, w_ref[g]) * s_ref[:, cols]).astype(o_ref.dtype)


def _pool_mix(pool_in, w_pool, pool_scale, segments):
    b, lc, pw = pool_in.shape
    nmax = max(n for _, n in segments)
    lo = lc - segments[0][0]
    assert sum(n for _, n in segments) == lo
    return pl.pallas_call(
        functools.partial(_pool_kernel, segments=segments),
        grid=(b,),
        in_specs=[
            pl.BlockSpec((1, lc, pw), lambda bi: (bi, 0, 0)),
            pl.BlockSpec(w_pool.shape, lambda bi: (0, 0, 0)),
            pl.BlockSpec(pool_scale.shape, lambda bi: (0, 0)),
        ],
        out_specs=pl.BlockSpec((1, lo, pw), lambda bi: (bi, 0, 0)),
        out_shape=jax.ShapeDtypeStruct((b, lo, pw), BF16),
        scratch_shapes=[pltpu.VMEM((nmax + 2 * POOL_HALO, POOL_GROUP_W), F32)],
        compiler_params=pltpu.CompilerParams(dimension_semantics=("parallel",), vmem_limit_bytes=VMEM_LIMIT),
        name="pool_mix",
    )(pool_in, w_pool, pool_scale)


def _merge_kernel(om_ref, op_ref, od_ref, gate_ref, h_ref, mod_ref, wb_ref, wo_ref, gpost_ref, gffn_ref, wr_ref,
                  h2_ref, fx_ref, lg_ref):
    d = h_ref.shape[-1]
    m = mod_ref[0, 0]
    merged = (gate_ref[0, :, 0:d].astype(F32) * _dot(om_ref[0], wb_ref[0])
              + gate_ref[0, :, d:2 * d].astype(F32) * _dot(op_ref[0], wb_ref[1])
              + gate_ref[0, :, 2 * d:3 * d].astype(F32) * _dot(od_ref[0], wb_ref[2]))
    y = _dot(merged.astype(BF16), wo_ref[...])
    h2 = h_ref[0] + m[2:3] * _rms(y, gpost_ref[...])
    h2_ref[0] = h2
    fx = _rms(h2, gffn_ref[...]) * (1.0 + m[4:5]) + m[3:4]
    fhi, flo = _split_bf16(fx)
    fx_ref[0] = fhi
    whi, wlo = _split_bf16(wr_ref[...])
    lg_ref[...] = _dot_nt(whi, fhi) + _dot_nt(whi, flo) + _dot_nt(wlo, fhi)


def _merge(o_mla, o_pool, o_diff, gates, h, mods, wb, w_out, g_post, g_ffn, w_rt, q_off):
    b, lc, d = h.shape
    nt = lc // TM - q_off
    ne = w_rt.shape[0]
    full = lambda w: pl.BlockSpec((1, TM, w), lambda bi, i: (bi, i + q_off, 0))
    tile = lambda w: pl.BlockSpec((1, TM, w), lambda bi, i: (bi, i, 0))
    return pl.pallas_call(
        _merge_kernel,
        grid=(b, nt),
        in_specs=[
            tile(o_mla.shape[-1]), tile(o_pool.shape[-1]), tile(o_diff.shape[-1]), full(gates.shape[-1]), full(d),
            pl.BlockSpec((1, 1, 6, d), lambda bi, i: (bi, jnp.minimum(i + q_off, 1), 0, 0)),
            _const_spec(wb.shape), _const_spec(w_out.shape), _const_spec(g_post.shape), _const_spec(g_ffn.shape),
            _const_spec(w_rt.shape),
        ],
        out_specs=[tile(d), tile(d), pl.BlockSpec((ne, TM), lambda bi, i: (0, bi * nt + i))],
        out_shape=[jax.ShapeDtypeStruct((b, nt * TM, d), F32), jax.ShapeDtypeStruct((b, nt * TM, d), BF16),
                   jax.ShapeDtypeStruct((ne, b * nt * TM), F32)],
        compiler_params=pltpu.CompilerParams(
            dimension_semantics=("parallel", "parallel"), vmem_limit_bytes=VMEM_LIMIT),
        name="merge",
    )(o_mla, o_pool, o_diff, gates, h, mods, wb, w_out, g_post, g_ffn, w_rt)


def _route_kernel(lg_ref, br_ref, rloc_ref, wk_ref, meta_ref, tot_ref, carry_ref):
    ne, tm = lg_ref.shape
    gsz = ne // N_GROUPS
    neg = -jnp.inf

    @pl.when(pl.program_id(0) == 0)
    def _():
        carry_ref[...] = jnp.zeros_like(carry_ref)

    scores = jax.nn.sigmoid(lg_ref[...])
    sel = scores + br_ref[...]

    jidx = lax.broadcasted_iota(I32, (gsz, tm), 0).astype(F32)
    gscore = []
    for g in range(N_GROUPS):
        v = sel[g * gsz:(g + 1) * gsz]
        m1 = jnp.max(v, axis=0, keepdims=True)
        first = jnp.min(jnp.where(v == m1, jidx, float(gsz)), axis=0, keepdims=True)
        m2 = jnp.max(jnp.where(jidx == first, neg, v), axis=0, keepdims=True)
        gscore.append(m1 + m2)
    masked = []
    for g in range(N_GROUPS):
        rank = jnp.zeros((1, tm), I32)
        for o in range(N_GROUPS):
            if o == g:
                continue
            ahead = gscore[o] > gscore[g]
            if o < g:
                ahead = ahead | (gscore[o] == gscore[g])
            rank = rank + ahead.astype(I32)
        masked.append(jnp.where(rank < TOPK_GROUPS, sel[g * gsz:(g + 1) * gsz], neg))
    cur = jnp.concatenate(masked, axis=0)

    eidx = lax.broadcasted_iota(I32, (ne, tm), 0).astype(F32)
    chosen = jnp.zeros((ne, tm), jnp.bool_)
    for _ in range(TOP_K):
        mx = jnp.max(cur, axis=0, keepdims=True)
        first = jnp.min(jnp.where(cur == mx, eidx, float(ne)), axis=0, keepdims=True)
        hit = eidx == first
        chosen = chosen | hit
        cur = jnp.where(hit, neg, cur)

    wsel = jnp.where(chosen, scores, 0.0)
    wts = wsel / jnp.sum(wsel, axis=0, keepdims=True) * ROUTED_SCALE

    cmask = chosen.astype(BF16)
    ti = lax.broadcasted_iota(I32, (tm, tm), 0)
    tj = lax.broadcasted_iota(I32, (tm, tm), 1)
    pos = _dot(cmask, (ti < tj).astype(BF16))
    ei = lax.broadcasted_iota(I32, (ne, ne), 0)
    ej = lax.broadcasted_iota(I32, (ne, ne), 1)
    kth = _dot((ej <= ei).astype(BF16), cmask)
    cnt = jnp.sum(chosen.astype(F32), axis=1, keepdims=True)
    cpad = jnp.floor((cnt + (ALIGN - 1)) * (1.0 / ALIGN)) * ALIGN
    cpad_b = jnp.broadcast_to(cpad, (ne, LANES))
    loc_b = _dot((ej < ei).astype(BF16), cpad_b.astype(BF16))
    row = loc_b[:, 0:1] + pos
    for k in range(TOP_K):
        pick = chosen & (kth == float(k + 1))
        rloc_ref[k:k + 1, :] = jnp.sum(jnp.where(pick, row, 0.0), axis=0, keepdims=True).astype(I32)
        wk_ref[k:k + 1, :] = jnp.sum(jnp.where(pick, wts, 0.0), axis=0, keepdims=True)

    gpos = carry_ref[...]
    lane = lax.broadcasted_iota(I32, (ne, LANES), 1)
    meta_ref[0] = jnp.where(lane == 0, cpad_b, jnp.where(lane == 1, loc_b, gpos)).astype(I32)
    carry_ref[...] = gpos + cpad_b
    tot_ref[...] = (gpos + cpad_b).astype(I32)


def _route(logits_t, b_router):
    ne, ntok = logits_t.shape
    ntiles = ntok // TM
    return pl.pallas_call(
        _route_kernel,
        grid=(ntiles,),
        in_specs=[pl.BlockSpec((ne, TM), lambda t: (0, t)), pl.BlockSpec((ne, 1), lambda t: (0, 0))],
        out_specs=[
            pl.BlockSpec((TOP_K, TM), lambda t: (0, t)),
            pl.BlockSpec((TOP_K, TM), lambda t: (0, t)),
            pl.BlockSpec((1, ne, LANES), lambda t: (t, 0, 0)),
            pl.BlockSpec((ne, LANES), lambda t: (0, 0)),
        ],
        out_shape=[
            jax.ShapeDtypeStruct((TOP_K, ntok), I32),
            jax.ShapeDtypeStruct((TOP_K, ntok), F32),
            jax.ShapeDtypeStruct((ntiles, ne, LANES), I32),
            jax.ShapeDtypeStruct((ne, LANES), I32),
        ],
        scratch_shapes=[pltpu.VMEM((ne, LANES), F32)],
        compiler_params=pltpu.CompilerParams(dimension_semantics=("arbitrary",)),
        name="route",
    )(logits_t, b_router)


def _segment_copies(act, n16_ref, loc_ref, seg_ref, t, local, remote, sem, to_remote):
    def per_expert(e, carry):
        l0 = loc_ref[t, e]
        s0 = seg_ref[t, e]

        def per_chunk(j, c):
            lrows = local.at[pl.ds(pl.multiple_of(l0 + j * ALIGN, ALIGN), ALIGN)]
            rrows = remote.at[pl.ds(pl.multiple_of(s0 + j * ALIGN, ALIGN), ALIGN)]
            cp = pltpu.make_async_copy(lrows, rrows, sem) if to_remote else pltpu.make_async_copy(rrows, lrows, sem)
            if act == 0:
                cp.start()
            else:
                cp.wait()
            return c

        lax.fori_loop(0, n16_ref[t, e], per_chunk, 0)
        return carry

    lax.fori_loop(0, N_EXPERTS, per_expert, 0)


def _dispatch_kernel(n16_ref, loc_ref, seg_ref, tail_ref, tailn_ref, nused_ref, fx_ref, rloc_ref, xs_ref, buf_ref,
                     zero_ref, sem):
    t = pl.program_id(0)
    tm = fx_ref.shape[1]
    x = fx_ref[0]
    for rc in range(RL // RCH):
        rows = lax.broadcasted_iota(I32, (RCH, tm), 0) + rc * RCH
        hit = rows == rloc_ref[0:1, :]
        for k in range(1, TOP_K):
            hit = hit | (rows == rloc_ref[k:k + 1, :])
        buf_ref[rc * RCH:(rc + 1) * RCH, :] = _dot(hit.astype(BF16), x).astype(buf_ref.dtype)

    _segment_copies(0, n16_ref, loc_ref, seg_ref, t, buf_ref, xs_ref, sem.at[0], True)

    last = t == pl.num_programs(0) - 1

    def tail(act):
        def go(cp):
            if act == 0:
                cp.start()
            else:
                cp.wait()

        def per_expert(e, carry):
            s0 = tail_ref[e]

            def per_chunk(j, c):
                go(pltpu.make_async_copy(
                    zero_ref.at[pl.ds(0, ALIGN)],
                    xs_ref.at[pl.ds(pl.multiple_of(s0 + j * ALIGN, ALIGN), ALIGN)], sem.at[1]))
                return c

            lax.fori_loop(0, tailn_ref[e], per_chunk, 0)
            return carry

        lax.fori_loop(0, N_EXPERTS, per_expert, 0)

        def per_block(j, c):
            go(pltpu.make_async_copy(zero_ref, xs_ref.at[pl.ds(pl.multiple_of(j * EBLK, EBLK), EBLK)], sem.at[1]))
            return c

        lax.fori_loop(nused_ref[0], xs_ref.shape[0] // EBLK, per_block, 0)

    @pl.when(last)
    def _():
        zero_ref[...] = jnp.zeros_like(zero_ref)
        tail(0)

    _segment_copies(1, n16_ref, loc_ref, seg_ref, t, buf_ref, xs_ref, sem.at[0], True)

    @pl.when(last)
    def _():
        tail(1)


def _dispatch(fx, rloc, n16, loc, seg, tail_start, tail_n16, n_used, rmax):
    b, rows, d = fx.shape
    nt = rows // TM
    ntiles = b * nt
    return pl.pallas_call(
        _dispatch_kernel,
        grid_spec=pltpu.PrefetchScalarGridSpec(
            num_scalar_prefetch=6,
            grid=(ntiles,),
            in_specs=[
                pl.BlockSpec((1, TM, d), lambda t, *_: (t // nt, t % nt, 0)),
                pl.BlockSpec((TOP_K, TM), lambda t, *_: (0, t)),
            ],
            out_specs=pl.BlockSpec(memory_space=pl.ANY),
            scratch_shapes=[pltpu.VMEM((RL, d), BF16), pltpu.VMEM((EBLK, d), BF16), pltpu.SemaphoreType.DMA((2,))],
        ),
        out_shape=jax.ShapeDtypeStruct((rmax, d), BF16),
        compiler_params=pltpu.CompilerParams(dimension_semantics=("arbitrary",), vmem_limit_bytes=VMEM_LIMIT),
        name="dispatch",
    )(n16, loc, seg, tail_start, tail_n16, n_used, fx, rloc)


def _expert_kernel(blk_ref, nused_ref, x_ref, wgu_ref, wd_ref, y_ref):
    @pl.when(pl.program_id(0) < nused_ref[0])
    def _():
        gu = _dot(x_ref[...], wgu_ref[0])
        hmid = (_silu(gu[:, :EXPERT_FF]) * gu[:, EXPERT_FF:]).astype(BF16)
        y_ref[...] = _dot(hmid, wd_ref[0]).astype(y_ref.dtype)


def _experts(xs, blk_e, n_used, w_gu, w_d):
    rmax, d = xs.shape
    nblk = rmax // EBLK
    row = lambda i, blk, nu: (jnp.minimum(i, nu[0] - 1), 0)
    wsel = lambda i, blk, nu: (blk[jnp.minimum(i, nu[0] - 1)], 0, 0)
    return pl.pallas_call(
        _expert_kernel,
        grid_spec=pltpu.PrefetchScalarGridSpec(
            num_scalar_prefetch=2,
            grid=(nblk,),
            in_specs=[
                pl.BlockSpec((EBLK, d), row),
                pl.BlockSpec((1, d, 2 * EXPERT_FF), wsel),
                pl.BlockSpec((1, EXPERT_FF, d), wsel),
            ],
            out_specs=pl.BlockSpec((EBLK, d), row),
        ),
        out_shape=jax.ShapeDtypeStruct((rmax, d), BF16),
        input_output_aliases={2: 0},
        compiler_params=pltpu.CompilerParams(dimension_semantics=("arbitrary",), vmem_limit_bytes=VMEM_LIMIT),
        name="experts",
    )(blk_e, n_used, xs, w_gu, w_d)


def _combine_kernel(n16_ref, loc_ref, seg_ref, y_ref, rloc_ref, wk_ref, fx_ref, h_ref, mod_ref, wsgu_ref, wsd_ref,
                    gpost_ref, o_ref, buf_ref, sem):
    t = pl.program_id(0)
    tm = fx_ref.shape[1]

    @pl.when(t == 0)
    def _():
        buf_ref[...] = jnp.zeros_like(buf_ref)

    _segment_copies(0, n16_ref, loc_ref, seg_ref, t, buf_ref, y_ref, sem.at[0], False)

    x = fx_ref[0]
    gu = _dot(x, wsgu_ref[...])
    ff = wsd_ref.shape[0]
    shared = _dot((_silu(gu[:, :ff]) * gu[:, ff:]).astype(BF16), wsd_ref[...])

    _segment_copies(1, n16_ref, loc_ref, seg_ref, t, buf_ref, y_ref, sem.at[0], False)

    routed = jnp.zeros_like(shared)
    for rc in range(RL // RCH):
        cols = lax.broadcasted_iota(I32, (tm, RCH), 1) + rc * RCH
        pw = jnp.zeros((tm, RCH), F32)
        for k in range(TOP_K):
            pw = pw + jnp.where(cols == rloc_ref[:, k:k + 1], wk_ref[:, k:k + 1], 0.0)
        phi, plo = _split_bf16(pw)
        ys = buf_ref[rc * RCH:(rc + 1) * RCH, :]
        routed = routed + _dot(phi, ys) + _dot(plo, ys)

    m = mod_ref[0, 0]
    o_ref[0] = h_ref[0] + m[5:6] * _rms(routed + shared, gpost_ref[...])


def _combine(y, rloc_t, wk_t, fx, h2, mods, n16, loc, seg, w_sgu, w_sd, g_post, q_off):
    b, rows, d = fx.shape
    nt = rows // TM
    ntiles = b * nt
    tile = lambda t, *_: (t // nt, t % nt, 0)
    return pl.pallas_call(
        _combine_kernel,
        grid_spec=pltpu.PrefetchScalarGridSpec(
            num_scalar_prefetch=3,
            grid=(ntiles,),
            in_specs=[
                pl.BlockSpec(memory_space=pl.ANY),
                pl.BlockSpec((TM, TOP_K), lambda t, *_: (t, 0)),
                pl.BlockSpec((TM, TOP_K), lambda t, *_: (t, 0)),
                pl.BlockSpec((1, TM, d), tile),
                pl.BlockSpec((1, TM, d), tile),
                pl.BlockSpec((1, 1, 6, d), lambda t, *_: (t // nt, jnp.minimum(t % nt + q_off, 1), 0, 0)),
                pl.BlockSpec(w_sgu.shape, lambda t, *_: (0, 0), pipeline_mode=pl.Buffered(1)),
                pl.BlockSpec(w_sd.shape, lambda t, *_: (0, 0), pipeline_mode=pl.Buffered(1)),
                pl.BlockSpec(g_post.shape, lambda t, *_: (0, 0), pipeline_mode=pl.Buffered(1)),
            ],
            out_specs=pl.BlockSpec((1, TM, d), tile),
            scratch_shapes=[pltpu.VMEM((RL, d), BF16), pltpu.SemaphoreType.DMA((1,))],
        ),
        out_shape=jax.ShapeDtypeStruct((b, rows, d), F32),
        compiler_params=pltpu.CompilerParams(dimension_semantics=("arbitrary",), vmem_limit_bytes=VMEM_LIMIT),
        name="combine",
    )(n16, loc, seg, y, rloc_t, wk_t, fx, h2, mods, w_sgu, w_sd, g_post)


def _rope_tables(n_lat, n_ctx, rot_dim, lane_slots):
    rows = n_lat // GRID_W
    pos_row = jnp.repeat(jnp.arange(rows, dtype=F32), GRID_W)
    pos_col = jnp.tile(jnp.arange(GRID_W, dtype=F32), rows)
    d_axis = rot_dim // 2
    inv_freq = ROPE_BASE ** (-jnp.arange(0, d_axis, 2, dtype=F32) / d_axis)
    ar = pos_row[:, None] * inv_freq
    ac = pos_col[:, None] * inv_freq
    ang = jnp.concatenate([ar, ar, ac, ac], axis=1)
    qtr = rot_dim // 4
    first = (jnp.arange(rot_dim) // qtr) % 2 == 0
    cos, sin = jnp.cos(ang), jnp.sin(ang)
    s1 = jnp.where(first, -sin, 0.0)
    s2 = jnp.where(first, 0.0, sin)
    lc = n_lat + n_ctx
    c_t = jnp.ones((lc, LANES), F32)
    s1_t = jnp.zeros((lc, LANES), F32)
    s2_t = jnp.zeros((lc, LANES), F32)
    for lo in lane_slots:
        c_t = c_t.at[n_ctx:, lo:lo + rot_dim].set(cos)
        s1_t = s1_t.at[n_ctx:, lo:lo + rot_dim].set(s1)
        s2_t = s2_t.at[n_ctx:, lo:lo + rot_dim].set(s2)
    return jnp.stack([c_t, s1_t, s2_t])


def _pack_layer_weights(w_in, w_uq, w_ukv):
    d = w_in.shape[0]
    kv0 = 0
    kr0 = kv0 + MLA_KV_RANK
    kd0 = kr0 + MLA_ROPE
    vd0 = kd0 + 512
    q0 = vd0 + 512
    qd0 = q0 + MLA_Q_RANK
    p0 = qd0 + 512
    g0 = p0 + 512
    kr = jnp.zeros((d, LANES), w_in.dtype).at[:, MLA_NOPE:MLA_NOPE + MLA_ROPE].set(w_in[:, kr0:kd0])
    w1 = jnp.concatenate([w_in[:, kv0:kr0], kr, w_in[:, kd0:vd0], w_in[:, vd0:q0], w_in[:, q0:qd0],
                          w_in[:, qd0:p0], w_in[:, p0:g0], w_in[:, g0:]], axis=1).astype(BF16)
    ukv = w_ukv.reshape(MLA_KV_RANK, MLA_HEADS, MLA_NOPE + MLA_V)
    uk = jnp.pad(ukv[:, :, :MLA_NOPE], ((0, 0), (0, 0), (0, LANES - MLA_NOPE))).reshape(MLA_KV_RANK, -1)
    uv = ukv[:, :, MLA_NOPE:].reshape(MLA_KV_RANK, -1)
    w_ukv_p = jnp.concatenate([uk, uv], axis=1).astype(BF16)
    uq = w_uq.reshape(MLA_Q_RANK, MLA_HEADS, MLA_NOPE + MLA_ROPE)
    w_uq_p = jnp.pad(uq, ((0, 0), (0, 0), (0, LANES - MLA_NOPE - MLA_ROPE))).reshape(MLA_Q_RANK, -1).astype(BF16)
    return w1, w_ukv_p, w_uq_p


def _routing_tables(meta, totals, rmax):
    cpad, loc, gpos = meta[:, :, 0], meta[:, :, 1], meta[:, :, 2]
    tot = totals[:, 0]
    reg = (tot + EBLK - 1) // EBLK * EBLK
    reg_end = jnp.cumsum(reg)
    reg_start = reg_end - reg
    seg = reg_start[None, :] + gpos
    n_used = jnp.maximum(reg_end[-1:] // EBLK, 1).astype(I32)
    blk_e = jnp.minimum(jnp.searchsorted(reg_end, jnp.arange(rmax // EBLK) * EBLK, side="right"),
                        N_EXPERTS - 1).astype(I32)
    return (cpad // ALIGN).astype(I32), loc.astype(I32), seg.astype(I32), (reg_start + tot).astype(I32), \
        ((reg - tot) // ALIGN).astype(I32), blk_e, n_used


def kernel(x, c, ctx, c_ctx, w_mod, b_mod, g_pre_mix, g_post_mix, g_pre_ffn, g_post_ffn, w_in, g_cq, w_uq, g_ckv, w_ukv, w_pool, pool_scale, lam_q1, lam_k1, lam_q2, lam_k2, g_subln, w_b_mla, w_b_pool, w_b_diff, w_out, w_router, b_router, w_e_gate, w_e_up, w_e_down, w_s_gate, w_s_up, w_s_down):
    b, n_lat, d = x.shape
    n_ctx = ctx.shape[1]
    depth = w_mod.shape[0]
    assert n_ctx == TM and n_lat % TM == 0 and n_lat % GRID_W == 0
    lc = n_ctx + n_lat
    row2 = lambda v: v.reshape(1, -1)

    rows = (b + 1 + 7) // 8 * 8
    cvec = jnp.zeros((rows, d), F32).at[:b].set(c).at[b].set(c_ctx)
    mod_all = _modulation(cvec, w_mod, b_mod).reshape(depth, rows, 6, d)

    tab_m = _rope_tables(n_lat, n_ctx, MLA_ROPE, (MLA_NOPE,))
    tab_d = _rope_tables(n_lat, n_ctx, DIFF_HALF, (0, DIFF_HALF))

    h = jnp.concatenate([ctx, x], axis=1)
    for l in range(depth):
        last = l == depth - 1
        q_off = 1 if last else 0
        lam_init = 0.8 - 0.6 * math.exp(-0.3 * l)
        mods = jnp.stack([jnp.broadcast_to(mod_all[l, b], (b, 6, d)), mod_all[l, :b]], axis=1)
        w1, w_ukv_p, w_uq_p = _pack_layer_weights(w_in[l], w_uq[l], w_ukv[l])

        qm, km, vm, qd, kd, vd, pool_in, gates = _project(
            h, mods, row2(g_pre_mix[l]), w1, row2(g_ckv[l]), w_ukv_p, row2(g_cq[l]), w_uq_p, tab_m, tab_d)
        o_mla = _mla_attention(qm, km, vm, q_off, n_ctx)
        lam_p = jnp.stack([lam_q1[l], lam_k1[l], lam_q2[l], lam_k2[l]])
        o_diff = _diff_attention(qd, kd, vd, lam_p, row2(g_subln[l]), q_off, n_ctx, lam_init)
        segments = ((n_ctx, n_lat),) if last else ((0, n_ctx), (n_ctx, n_lat))
        o_pool = _pool_mix(pool_in, w_pool[l].astype(BF16), row2(pool_scale[l]), segments)

        wb = jnp.stack([w_b_mla[l], w_b_pool[l], w_b_diff[l]]).astype(BF16)
        h2, fx, logits_t = _merge(o_mla, o_pool, o_diff, gates, h, mods, wb, w_out[l].astype(BF16),
                                  row2(g_post_mix[l]), row2(g_pre_ffn[l]), w_router[l].T, q_off)

        rloc, wk, meta, totals = _route(logits_t, b_router[l].reshape(-1, 1))
        ntiles = meta.shape[0]
        rmax = ntiles * RL + N_EXPERTS * EBLK
        n16, loc, seg, tail_start, tail_n16, blk_e, n_used = _routing_tables(meta, totals, rmax)
        xs = _dispatch(fx, rloc, n16, loc, seg, tail_start, tail_n16, n_used, rmax)
        w_gu = jnp.concatenate([w_e_gate[l], w_e_up[l]], axis=-1).astype(BF16)
        ys = _experts(xs, blk_e, n_used, w_gu, w_e_down[l].astype(BF16))
        w_sgu = jnp.concatenate([w_s_gate[l], w_s_up[l]], axis=-1).astype(BF16)
        h = _combine(ys, rloc.T, wk.T, fx, h2, mods, n16, loc, seg, w_sgu, w_s_down[l].astype(BF16),
                     row2(g_post_ffn[l]), q_off)
    return h
```

```python
import functools
import math

import jax
import jax.numpy as jnp
from jax import lax
from jax.experimental import pallas as pl
from jax.experimental.pallas import tpu as pltpu

F32 = jnp.float32
BF16 = jnp.bfloat16
I32 = jnp.int32

EPS = 1e-6
ROPE_BASE = 10000.0
GRID_W = 64

MLA_HEADS = 8
MLA_NOPE = 64
MLA_ROPE = 32
MLA_V = 64
MLA_Q_RANK = 384
MLA_KV_RANK = 256
POOL_WINDOWS = (2, 4, 8, 16)
POOL_GROUP_W = 128
DIFF_HEADS = 4
DIFF_HALF = 64
DIFF_V = 128
N_EXPERTS = 64
TOP_K = 8
N_GROUPS = 8
TOPK_GROUPS = 4
EXPERT_FF = 256
ROUTED_SCALE = 2.5

LANES = 128
TM = 256
TQ = 512
ALIGN = 16
EBLK = 512
RL = TM * TOP_K + N_EXPERTS * ALIGN
RCH = 512
VMEM_LIMIT = 56 * 1024 * 1024
LOG2E = 1.4426950408889634

C_CKV = 0
C_KR = C_CKV + MLA_KV_RANK
C_KD = C_KR + LANES
C_VD = C_KD + 512
C_CQ = C_VD + 512
C_QD = C_CQ + MLA_Q_RANK
C_POOL = C_QD + 512
C_GATE = C_POOL + 512


def _rms(x, g):
    ms = jnp.mean(x * x, axis=-1, keepdims=True)
    return x * lax.rsqrt(ms + EPS) * g


def _dot(a, b):
    return jnp.dot(a, b, preferred_element_type=F32)


def _dot_nt(a, b):
    return lax.dot_general(a, b, (((1,), (1,)), ((), ())), preferred_element_type=F32)


def _split_bf16(x):
    hi = x.astype(BF16)
    lo = (x - hi.astype(F32)).astype(BF16)
    return hi, lo


def _silu(x):
    return x * jax.nn.sigmoid(x)


def _rope(x, tab, quarter):
    w = x.shape[-1]
    rep = w // LANES
    c, s1, s2 = (jnp.tile(t, (1, rep)) if rep > 1 else t for t in tab)
    return x * c + pltpu.roll(x, w - quarter, 1) * s1 + pltpu.roll(x, quarter, 1) * s2


def _const_spec(shape):
    nd = len(shape)
    return pl.BlockSpec(shape, lambda *_: (0,) * nd, pipeline_mode=pl.Buffered(1))


def _mod_kernel(a_ref, w_ref, b_ref, o_ref):
    a = _silu(a_ref[...])
    ahi, alo = _split_bf16(a)
    whi, wlo = _split_bf16(w_ref[...])
    o_ref[...] = _dot(ahi, whi) + _dot(alo, whi) + _dot(ahi, wlo) + b_ref[...]


def _modulation(cvec, w_mod, b_mod):
    nl, d, d6 = w_mod.shape
    rows = cvec.shape[0]
    nc = d6 // d
    return pl.pallas_call(
        _mod_kernel,
        grid=(nl, nc),
        in_specs=[
            pl.BlockSpec((rows, d), lambda l, j: (0, 0)),
            pl.BlockSpec((None, d, d), lambda l, j: (l, 0, j)),
            pl.BlockSpec((None, 1, d), lambda l, j: (l, 0, j)),
        ],
        out_specs=pl.BlockSpec((None, rows, d), lambda l, j: (l, 0, j)),
        out_shape=jax.ShapeDtypeStruct((nl, rows, d6), F32),
        name="modulation",
    )(cvec, w_mod, b_mod.reshape(nl, 1, d6))


def _mod_spec(d, n_lat_tiles):
    return pl.BlockSpec((1, 1, 6, d), lambda bi, i: (bi, jnp.where(i < n_lat_tiles, 1, 0), 0, 0))


def _proj_kernel(h_ref, mod_ref, gpre_ref, w1_ref, gckv_ref, wukv_ref, gcq_ref, wuq_ref, tm_ref, td_ref,
                 qm_ref, km_ref, vm_ref, qd_ref, kd_ref, vd_ref, pool_ref, gate_ref):
    d = h_ref.shape[-1]
    m = mod_ref[0, 0]
    hx = (_rms(h_ref[0], gpre_ref[...]) * (1.0 + m[1:2]) + m[0:1]).astype(BF16)
    tabm = (tm_ref[0], tm_ref[1], tm_ref[2])
    tabd = (td_ref[0], td_ref[1], td_ref[2])

    ckv = _dot(hx, w1_ref[:, C_CKV:C_KR])
    kv = _dot(_rms(ckv, gckv_ref[...]).astype(BF16), wukv_ref[...])
    kr = _rope(_dot(hx, w1_ref[:, C_KR:C_KD]), tabm, MLA_ROPE // 4)
    nk = MLA_HEADS * LANES
    km_ref[0] = (kv[:, :nk] + jnp.tile(kr, (1, MLA_HEADS))).astype(BF16)
    vm_ref[0] = kv[:, nk:].astype(BF16)

    kd_ref[0] = _rope(_dot(hx, w1_ref[:, C_KD:C_VD]), tabd, DIFF_HALF // 4).astype(BF16)
    vd_ref[0] = _dot(hx, w1_ref[:, C_VD:C_CQ]).astype(BF16)

    cq = _dot(hx, w1_ref[:, C_CQ:C_QD])
    q = _dot(_rms(cq, gcq_ref[...]).astype(BF16), wuq_ref[...])
    qm_scale = LOG2E / math.sqrt(MLA_NOPE + MLA_ROPE)
    qm_ref[0] = (_rope(q, tabm, MLA_ROPE // 4) * qm_scale).astype(BF16)
    qd_scale = LOG2E / math.sqrt(DIFF_HALF)
    qd_ref[0] = (_rope(_dot(hx, w1_ref[:, C_QD:C_POOL]), tabd, DIFF_HALF // 4) * qd_scale).astype(BF16)
    pool_ref[0] = _dot(hx, w1_ref[:, C_POOL:C_GATE]).astype(BF16)
    for j in range(3):
        lo = C_GATE + j * d
        gate_ref[0, :, j * d:(j + 1) * d] = jax.nn.sigmoid(_dot(hx, w1_ref[:, lo:lo + d])).astype(BF16)


def _project(h, mods, g_pre, w1, g_ckv, w_ukv, g_cq, w_uq, tab_m, tab_d, n_lat):
    b, lc, d = h.shape
    nt = lc // TM
    tile = lambda w: pl.BlockSpec((1, TM, w), lambda bi, i: (bi, i, 0))
    widths = (MLA_HEADS * LANES, MLA_HEADS * LANES, MLA_HEADS * MLA_V, 512, 512, 512, 512, 3 * d)
    return pl.pallas_call(
        _proj_kernel,
        grid=(b, nt),
        in_specs=[
            tile(d),
            _mod_spec(d, n_lat // TM),
            _const_spec(g_pre.shape),
            _const_spec(w1.shape),
            _const_spec(g_ckv.shape),
            _const_spec(w_ukv.shape),
            _const_spec(g_cq.shape),
            _const_spec(w_uq.shape),
            pl.BlockSpec((3, TM, LANES), lambda bi, i: (0, i, 0)),
            pl.BlockSpec((3, TM, LANES), lambda bi, i: (0, i, 0)),
        ],
        out_specs=[tile(w) for w in widths],
        out_shape=[jax.ShapeDtypeStruct((b, lc, w), BF16) for w in widths],
        compiler_params=pltpu.CompilerParams(
            dimension_semantics=("parallel", "parallel"), vmem_limit_bytes=VMEM_LIMIT),
        name="project",
    )(h, mods, g_pre, w1, g_ckv, w_ukv, g_cq, w_uq, tab_m, tab_d)


def _softmax_pv(q, k, vext):
    s = _dot_nt(q, k)
    e = jnp.exp2(s - jnp.max(s, axis=-1, keepdims=True)).astype(BF16)
    pv = _dot(e, vext)
    w = vext.shape[-1] // 2
    return pv[:, :w], pv[:, w:w + 1]


def _fill_vext(v_ref, vext_ref):
    @pl.when(pl.program_id(2) == 0)
    def _():
        w = v_ref.shape[-1]
        vext_ref[:, :w] = v_ref[0]
        vext_ref[:, w:] = jnp.ones((vext_ref.shape[0], vext_ref.shape[1] - w), vext_ref.dtype)


def _attend_steps(attend, n_lat_steps, n_lat, lc, refs_lat, refs_ctx):
    i = pl.program_id(2)
    if refs_ctx is None:
        attend(*refs_lat, 0, lc)
    else:
        pl.when(i < n_lat_steps)(lambda: attend(*refs_lat, 0, lc))
        pl.when(i == n_lat_steps)(lambda: attend(*refs_ctx, n_lat, lc))


def _mla_kernel(*refs, n_lat, has_ctx):
    if has_ctx:
        ql_ref, qc_ref, k_ref, v_ref, ol_ref, oc_ref, vext_ref = refs
    else:
        ql_ref, k_ref, v_ref, ol_ref, vext_ref = refs
    lc = k_ref.shape[1]
    lane = lax.broadcasted_iota(I32, (1, LANES), 1)
    _fill_vext(v_ref, vext_ref)

    def attend(q_ref, o_ref, k0, k1):
        outs = []
        for j in range(2):
            o, l = _softmax_pv(q_ref[0, :, j * LANES:(j + 1) * LANES], k_ref[0, k0:k1, j * LANES:(j + 1) * LANES],
                               vext_ref[k0:k1, :])
            outs.append(o * (1.0 / l))
        o_ref[0] = jnp.where(lane < MLA_V, outs[0], outs[1]).astype(o_ref.dtype)

    _attend_steps(attend, n_lat // TQ, n_lat, lc, (ql_ref, ol_ref), (qc_ref, oc_ref) if has_ctx else None)


def _diff_kernel(*refs, n_lat, has_ctx, lam_init):
    if has_ctx:
        ql_ref, qc_ref, k_ref, v_ref, lam_ref, g_ref, ol_ref, oc_ref, vext_ref = refs
    else:
        ql_ref, k_ref, v_ref, lam_ref, g_ref, ol_ref, vext_ref = refs
    lc = k_ref.shape[1]
    lane = lax.broadcasted_iota(I32, (1, LANES), 1)
    lp = lam_ref[...]
    lam = (jnp.exp(jnp.sum(lp[0:1] * lp[1:2], axis=-1, keepdims=True))
           - jnp.exp(jnp.sum(lp[2:3] * lp[3:4], axis=-1, keepdims=True)) + lam_init)
    _fill_vext(v_ref, vext_ref)

    def attend(q_ref, o_ref, k0, k1):
        q = q_ref[0]
        k = k_ref[0, k0:k1, :]
        vext = vext_ref[k0:k1, :]
        zero = jnp.zeros_like(q)
        o0, l0 = _softmax_pv(jnp.where(lane < DIFF_HALF, q, zero), k, vext)
        o1, l1 = _softmax_pv(jnp.where(lane < DIFF_HALF, zero, q), k, vext)
        o = o0 * (1.0 / l0) - o1 * (lam / l1)
        o_ref[0] = (_rms(o, g_ref[...]) * (1.0 - lam_init)).astype(o_ref.dtype)

    _attend_steps(attend, n_lat // TQ, n_lat, lc, (ql_ref, ol_ref), (qc_ref, oc_ref) if has_ctx else None)


def _attention(body, name, q, k, v, extras, n_lat, has_ctx, heads, wq, wv, wo):
    b, lc, _ = q.shape
    n_ctx = lc - n_lat
    nql = n_lat // TQ
    lat = lambda w: pl.BlockSpec((1, TQ, w), lambda bi, h, i: (bi, jnp.minimum(i, nql - 1), h))
    ctx_in = pl.BlockSpec((1, n_ctx, wq), lambda bi, h, i: (bi, n_lat // n_ctx, h))
    ctx_out = pl.BlockSpec((1, n_ctx, wo), lambda bi, h, i: (bi, 0, h))
    full = lambda w: pl.BlockSpec((1, lc, w), lambda bi, h, i: (bi, 0, h))
    small = [pl.BlockSpec(a.shape, lambda bi, h, i: (0, 0)) for a in extras]
    in_specs = [lat(wq)] + ([ctx_in] if has_ctx else []) + [full(wq), full(wv)] + small
    args = [q] + ([q] if has_ctx else []) + [k, v] + list(extras)
    out_specs = [lat(wo)] + ([ctx_out] if has_ctx else [])
    out_shape = [jax.ShapeDtypeStruct((b, n_lat, heads * wo), BF16)] + (
        [jax.ShapeDtypeStruct((b, n_ctx, heads * wo), BF16)] if has_ctx else [])
    outs = pl.pallas_call(
        body,
        grid=(b, heads, nql + (1 if has_ctx else 0)),
        in_specs=in_specs,
        out_specs=out_specs,
        out_shape=out_shape,
        scratch_shapes=[pltpu.VMEM((lc, 2 * wv), BF16)],
        compiler_params=pltpu.CompilerParams(
            dimension_semantics=("parallel", "parallel", "arbitrary"), vmem_limit_bytes=VMEM_LIMIT),
        name=name,
    )(*args)
    return outs if has_ctx else (outs[0], None)


POOL_HALO = 16


def _pool_kernel(u_ref, w_ref, s_ref, o_ref, pad_ref, *, segments):
    for s0, n in segments:
        t = lax.broadcasted_iota(I32, (n, 1), 0)
        for g, w in enumerate(POOL_WINDOWS):
            cols = slice(g * POOL_GROUP_W, (g + 1) * POOL_GROUP_W)
            xg = u_ref[0, s0:s0 + n, cols].astype(F32)
            zeros = jnp.zeros((POOL_HALO, POOL_GROUP_W), F32)
            pad_ref[0:POOL_HALO, :] = zeros
            pad_ref[POOL_HALO + n:2 * POOL_HALO + n, :] = zeros
            pad_ref[POOL_HALO:POOL_HALO + n, :] = xg
            acc = pad_ref[POOL_HALO - w // 2:POOL_HALO - w // 2 + n, :]
            for j in range(1 - w // 2, w // 2):
                acc = acc + pad_ref[POOL_HALO + j:POOL_HALO + j + n, :]
            cnt = (jnp.minimum(t + w // 2, n) - jnp.maximum(t - w // 2, 0)).astype(F32)
            pooled = (acc / cnt - xg).astype(BF16)
            o_ref[0, s0:s0 + n, cols] = (_dot(pooled, w_ref[g]) * s_ref[:, cols]).astype(o_ref.dtype)


def _pool_mix(pool_in, w_pool, pool_scale, segments):
    b, lc, pw = pool_in.shape
    nmax = max(n for _, n in segments)
    lo = sum(n for _, n in segments)
    return pl.pallas_call(
        functools.partial(_pool_kernel, segments=segments),
        grid=(b,),
        in_specs=[
            pl.BlockSpec((1, lc, pw), lambda bi: (bi, 0, 0)),
            pl.BlockSpec(w_pool.shape, lambda bi: (0, 0, 0)),
            pl.BlockSpec(pool_scale.shape, lambda bi: (0, 0)),
        ],
        out_specs=pl.BlockSpec((1, lo, pw), lambda bi: (bi, 0, 0)),
        out_shape=jax.ShapeDtypeStruct((b, lo, pw), BF16),
        scratch_shapes=[pltpu.VMEM((nmax + 2 * POOL_HALO, POOL_GROUP_W), F32)],
        compiler_params=pltpu.CompilerParams(dimension_semantics=("parallel",), vmem_limit_bytes=VMEM_LIMIT),
        name="pool_mix",
    )(pool_in, w_pool, pool_scale)


def _merge_kernel(*refs, n_lat_tiles, has_ctx):
    if has_ctx:
        (oml_ref, omc_ref, odl_ref, odc_ref, op_ref, gate_ref, h_ref, mod_ref, wb_ref, wo_ref, gpost_ref, gffn_ref,
         wr_ref, h2_ref, fx_ref, lg_ref) = refs
        is_ctx = pl.program_id(1) >= n_lat_tiles
        om = jnp.where(is_ctx, omc_ref[0], oml_ref[0])
        od = jnp.where(is_ctx, odc_ref[0], odl_ref[0])
    else:
        (oml_ref, odl_ref, op_ref, gate_ref, h_ref, mod_ref, wb_ref, wo_ref, gpost_ref, gffn_ref,
         wr_ref, h2_ref, fx_ref, lg_ref) = refs
        om = oml_ref[0]
        od = odl_ref[0]
    d = h_ref.shape[-1]
    m = mod_ref[0, 0]
    merged = (gate_ref[0, :, 0:d].astype(F32) * _dot(om, wb_ref[0])
              + gate_ref[0, :, d:2 * d].astype(F32) * _dot(op_ref[0], wb_ref[1])
              + gate_ref[0, :, 2 * d:3 * d].astype(F32) * _dot(od, wb_ref[2]))
    y = _dot(merged.astype(BF16), wo_ref[...])
    h2 = h_ref[0] + m[2:3] * _rms(y, gpost_ref[...])
    h2_ref[0] = h2
    fx = _rms(h2, gffn_ref[...]) * (1.0 + m[4:5]) + m[3:4]
    fhi, flo = _split_bf16(fx)
    fx_ref[0] = fhi
    whi, wlo = _split_bf16(wr_ref[...])
    lg_ref[...] = _dot_nt(whi, fhi) + _dot_nt(whi, flo) + _dot_nt(wlo, fhi)


def _merge(o_mla, o_mla_ctx, o_diff, o_diff_ctx, o_pool, gates, h, mods, wb, w_out, g_post, g_ffn, w_rt, n_lat, nt):
    b, _, d = h.shape
    ne = w_rt.shape[0]
    nlt = n_lat // TM
    has_ctx = o_mla_ctx is not None
    tile = lambda w: pl.BlockSpec((1, TM, w), lambda bi, i: (bi, i, 0))
    lat = lambda w: pl.BlockSpec((1, TM, w), lambda bi, i: (bi, jnp.minimum(i, nlt - 1), 0))
    ctx = lambda w: pl.BlockSpec((1, TM, w), lambda bi, i: (bi, 0, 0))
    wm, wd = o_mla.shape[-1], o_diff.shape[-1]
    if has_ctx:
        attn_specs = [lat(wm), ctx(wm), lat(wd), ctx(wd)]
        attn_args = [o_mla, o_mla_ctx, o_diff, o_diff_ctx]
    else:
        attn_specs = [tile(wm), tile(wd)]
        attn_args = [o_mla, o_diff]
    return pl.pallas_call(
        functools.partial(_merge_kernel, n_lat_tiles=nlt, has_ctx=has_ctx),
        grid=(b, nt),
        in_specs=attn_specs + [
            tile(o_pool.shape[-1]), tile(gates.shape[-1]), tile(d), _mod_spec(d, nlt),
            _const_spec(wb.shape), _const_spec(w_out.shape), _const_spec(g_post.shape), _const_spec(g_ffn.shape),
            _const_spec(w_rt.shape),
        ],
        out_specs=[tile(d), tile(d), pl.BlockSpec((ne, TM), lambda bi, i: (0, bi * nt + i))],
        out_shape=[jax.ShapeDtypeStruct((b, nt * TM, d), F32), jax.ShapeDtypeStruct((b, nt * TM, d), BF16),
                   jax.ShapeDtypeStruct((ne, b * nt * TM), F32)],
        compiler_params=pltpu.CompilerParams(
            dimension_semantics=("parallel", "parallel"), vmem_limit_bytes=VMEM_LIMIT),
        name="merge",
    )(*attn_args, o_pool, gates, h, mods, wb, w_out, g_post, g_ffn, w_rt)


def _route_kernel(lg_ref, br_ref, posm_ref, wts_ref, meta_ref, tot_ref, carry_ref):
    ne, tm = lg_ref.shape
    gsz = ne // N_GROUPS
    neg = -jnp.inf

    @pl.when(pl.program_id(0) == 0)
    def _():
        carry_ref[...] = jnp.zeros_like(carry_ref)

    scores = jax.nn.sigmoid(lg_ref[...])
    sel = scores + br_ref[...]

    jidx = lax.broadcasted_iota(I32, (gsz, tm), 0).astype(F32)
    gscore = []
    for g in range(N_GROUPS):
        v = sel[g * gsz:(g + 1) * gsz]
        m1 = jnp.max(v, axis=0, keepdims=True)
        first = jnp.min(jnp.where(v == m1, jidx, float(gsz)), axis=0, keepdims=True)
        m2 = jnp.max(jnp.where(jidx == first, neg, v), axis=0, keepdims=True)
        gscore.append(m1 + m2)
    masked = []
    for g in range(N_GROUPS):
        rank = jnp.zeros((1, tm), I32)
        for o in range(N_GROUPS):
            if o == g:
                continue
            ahead = gscore[o] > gscore[g]
            if o < g:
                ahead = ahead | (gscore[o] == gscore[g])
            rank = rank + ahead.astype(I32)
        masked.append(jnp.where(rank < TOPK_GROUPS, sel[g * gsz:(g + 1) * gsz], neg))
    cur = jnp.concatenate(masked, axis=0)

    eidx = lax.broadcasted_iota(I32, (ne, tm), 0).astype(F32)
    chosen = jnp.zeros((ne, tm), jnp.bool_)
    for _ in range(TOP_K):
        mx = jnp.max(cur, axis=0, keepdims=True)
        first = jnp.min(jnp.where(cur == mx, eidx, float(ne)), axis=0, keepdims=True)
        hit = eidx == first
        chosen = chosen | hit
        cur = jnp.where(hit, neg, cur)

    wsel = jnp.where(chosen, scores, 0.0)
    wts_ref[...] = wsel / jnp.sum(wsel, axis=0, keepdims=True) * ROUTED_SCALE

    cmask = chosen.astype(BF16)
    ti = lax.broadcasted_iota(I32, (tm, tm), 0)
    tj = lax.broadcasted_iota(I32, (tm, tm), 1)
    pos = _dot(cmask, (ti < tj).astype(BF16))
    posm_ref[...] = jnp.where(chosen, pos, -1.0)
    ei = lax.broadcasted_iota(I32, (ne, ne), 0)
    ej = lax.broadcasted_iota(I32, (ne, ne), 1)
    cnt = jnp.sum(chosen.astype(F32), axis=1, keepdims=True)
    cpad = jnp.floor((cnt + (ALIGN - 1)) * (1.0 / ALIGN)) * ALIGN
    cpad_b = jnp.broadcast_to(cpad, (ne, LANES))
    loc_b = _dot((ej < ei).astype(BF16), cpad_b.astype(BF16))

    gpos = carry_ref[...]
    lane = lax.broadcasted_iota(I32, (ne, LANES), 1)
    meta_ref[0] = jnp.where(lane == 0, cpad_b, jnp.where(lane == 1, loc_b, gpos)).astype(I32)
    carry_ref[...] = gpos + cpad_b
    tot_ref[...] = (gpos + cpad_b).astype(I32)


def _route(logits_t, b_router):
    ne, ntok = logits_t.shape
    ntiles = ntok // TM
    return pl.pallas_call(
        _route_kernel,
        grid=(ntiles,),
        in_specs=[pl.BlockSpec((ne, TM), lambda t: (0, t)), pl.BlockSpec((ne, 1), lambda t: (0, 0))],
        out_specs=[
            pl.BlockSpec((ne, TM), lambda t: (0, t)),
            pl.BlockSpec((ne, TM), lambda t: (0, t)),
            pl.BlockSpec((1, ne, LANES), lambda t: (t, 0, 0)),
            pl.BlockSpec((ne, LANES), lambda t: (0, 0)),
        ],
        out_shape=[
            jax.ShapeDtypeStruct((ne, ntok), F32),
            jax.ShapeDtypeStruct((ne, ntok), F32),
            jax.ShapeDtypeStruct((ntiles, ne, LANES), I32),
            jax.ShapeDtypeStruct((ne, LANES), I32),
        ],
        scratch_shapes=[pltpu.VMEM((ne, LANES), F32)],
        compiler_params=pltpu.CompilerParams(dimension_semantics=("arbitrary",)),
        name="route",
    )(logits_t, b_router)


def _segment_copies(act, n16_ref, loc_ref, seg_ref, t, local, remote, sem, to_remote):
    def per_expert(e, carry):
        l0 = loc_ref[t, e]
        s0 = seg_ref[t, e]

        def per_chunk(j, c):
            lrows = local.at[pl.ds(pl.multiple_of(l0 + j * ALIGN, ALIGN), ALIGN)]
            rrows = remote.at[pl.ds(pl.multiple_of(s0 + j * ALIGN, ALIGN), ALIGN)]
            cp = pltpu.make_async_copy(lrows, rrows, sem) if to_remote else pltpu.make_async_copy(rrows, lrows, sem)
            if act == 0:
                cp.start()
            else:
                cp.wait()
            return c

        lax.fori_loop(0, n16_ref[t, e], per_chunk, 0)
        return carry

    lax.fori_loop(0, N_EXPERTS, per_expert, 0)


def _dispatch_kernel(n16_ref, loc_ref, seg_ref, tail_ref, tailn_ref, nused_ref, fx_ref, posm_ref, lohi_ref, xs_ref,
                     buf_ref, zero_ref, sem):
    t = pl.program_id(0)
    slot = lax.rem(t, 2)
    x = fx_ref[0]
    buf = buf_ref.at[slot]
    posm = posm_ref[...]
    m2 = jnp.concatenate([posm, jnp.zeros((LANES - posm.shape[0], posm.shape[1]), F32)], axis=0).astype(BF16)
    lo = lohi_ref[0, 0:1, :]
    hi = lohi_ref[0, 1:2, :]
    for rc in range(RL // RCH):
        r = (lax.broadcasted_iota(I32, (RCH, 1), 0) + rc * RCH).astype(F32)
        member = (r >= lo) & (r < hi)
        base = jnp.sum(jnp.where(member, lo, 0.0), axis=1, keepdims=True)
        onehot = (_dot(member.astype(BF16), m2) == r - base).astype(BF16)
        buf[rc * RCH:(rc + 1) * RCH, :] = _dot(onehot, x).astype(buf_ref.dtype)

    @pl.when(t > 0)
    def _():
        _segment_copies(1, n16_ref, loc_ref, seg_ref, t - 1, buf_ref.at[1 - slot], xs_ref, sem.at[0], True)

    _segment_copies(0, n16_ref, loc_ref, seg_ref, t, buf, xs_ref, sem.at[0], True)

    def tail(act):
        def go(cp):
            if act == 0:
                cp.start()
            else:
                cp.wait()

        def per_expert(e, carry):
            s0 = tail_ref[e]

            def per_chunk(j, c):
                go(pltpu.make_async_copy(
                    zero_ref.at[pl.ds(0, ALIGN)],
                    xs_ref.at[pl.ds(pl.multiple_of(s0 + j * ALIGN, ALIGN), ALIGN)], sem.at[1]))
                return c

            lax.fori_loop(0, tailn_ref[e], per_chunk, 0)
            return carry

        lax.fori_loop(0, N_EXPERTS, per_expert, 0)

        def per_block(j, c):
            go(pltpu.make_async_copy(zero_ref, xs_ref.at[pl.ds(pl.multiple_of(j * EBLK, EBLK), EBLK)], sem.at[1]))
            return c

        lax.fori_loop(nused_ref[0], xs_ref.shape[0] // EBLK, per_block, 0)

    @pl.when(t == pl.num_programs(0) - 1)
    def _():
        zero_ref[...] = jnp.zeros_like(zero_ref)
        tail(0)
        _segment_copies(1, n16_ref, loc_ref, seg_ref, t, buf, xs_ref, sem.at[0], True)
        tail(1)


def _dispatch(fx, posm, lohi, tabs, rmax):
    b, rows, d = fx.shape
    nt = rows // TM
    ntiles = b * nt
    return pl.pallas_call(
        _dispatch_kernel,
        grid_spec=pltpu.PrefetchScalarGridSpec(
            num_scalar_prefetch=6,
            grid=(ntiles,),
            in_specs=[
                pl.BlockSpec((1, TM, d), lambda t, *_: (t // nt, t % nt, 0)),
                pl.BlockSpec((posm.shape[0], TM), lambda t, *_: (0, t)),
                pl.BlockSpec((1,) + lohi.shape[1:], lambda t, *_: (t, 0, 0)),
            ],
            out_specs=pl.BlockSpec(memory_space=pl.ANY),
            scratch_shapes=[pltpu.VMEM((2, RL, d), BF16), pltpu.VMEM((EBLK, d), BF16),
                            pltpu.SemaphoreType.DMA((2,))],
        ),
        out_shape=jax.ShapeDtypeStruct((rmax, d), BF16),
        compiler_params=pltpu.CompilerParams(dimension_semantics=("arbitrary",), vmem_limit_bytes=VMEM_LIMIT),
        name="dispatch",
    )(tabs["n16"], tabs["loc"], tabs["seg"], tabs["tail_start"], tabs["tail_n16"], tabs["n_used"], fx, posm, lohi)


def _expert_kernel(blk_ref, nused_ref, x_ref, wgu_ref, wd_ref, y_ref):
    @pl.when(pl.program_id(0) < nused_ref[0])
    def _():
        gu = _dot(x_ref[...], wgu_ref[0])
        hmid = (_silu(gu[:, :EXPERT_FF]) * gu[:, EXPERT_FF:]).astype(BF16)
        y_ref[...] = _dot(hmid, wd_ref[0]).astype(y_ref.dtype)


def _experts(xs, blk_e, n_used, w_gu, w_d):
    rmax, d = xs.shape
    nblk = rmax // EBLK
    row = lambda i, blk, nu: (jnp.minimum(i, nu[0] - 1), 0)
    wsel = lambda i, blk, nu: (blk[jnp.minimum(i, nu[0] - 1)], 0, 0)
    return pl.pallas_call(
        _expert_kernel,
        grid_spec=pltpu.PrefetchScalarGridSpec(
            num_scalar_prefetch=2,
            grid=(nblk,),
            in_specs=[
                pl.BlockSpec((EBLK, d), row),
                pl.BlockSpec((1, d, 2 * EXPERT_FF), wsel),
                pl.BlockSpec((1, EXPERT_FF, d), wsel),
            ],
            out_specs=pl.BlockSpec((EBLK, d), row),
        ),
        out_shape=jax.ShapeDtypeStruct((rmax, d), BF16),
        input_output_aliases={2: 0},
        compiler_params=pltpu.CompilerParams(dimension_semantics=("arbitrary",), vmem_limit_bytes=VMEM_LIMIT),
        name="experts",
    )(blk_e, n_used, xs, w_gu, w_d)


def _combine_kernel(n16_ref, loc_ref, seg_ref, y_ref, posmt_ref, wt_ref, meta_ref, fx_ref, h_ref, mod_ref, wsgu_ref,
                    wsd_ref, gpost_ref, o_ref, buf_ref, sem):
    t = pl.program_id(0)
    slot = lax.rem(t, 2)
    buf = buf_ref.at[slot]

    @pl.when(t == 0)
    def _():
        buf_ref[...] = jnp.zeros_like(buf_ref)
        _segment_copies(0, n16_ref, loc_ref, seg_ref, t, buf, y_ref, sem.at[slot], False)

    @pl.when(t + 1 < pl.num_programs(0))
    def _():
        _segment_copies(0, n16_ref, loc_ref, seg_ref, t + 1, buf_ref.at[1 - slot], y_ref, sem.at[1 - slot], False)

    x = fx_ref[0]
    gu = _dot(x, wsgu_ref[...])
    ff = wsd_ref.shape[0]
    o_ref[0] = _dot((_silu(gu[:, :ff]) * gu[:, ff:]).astype(BF16), wsd_ref[...])

    _segment_copies(1, n16_ref, loc_ref, seg_ref, t, buf, y_ref, sem.at[slot], False)

    posmt = posmt_ref[...].astype(BF16)
    wt = wt_ref[...].astype(BF16)
    lo = meta_ref[0, :, 1:2].astype(F32)
    hi = lo + meta_ref[0, :, 0:1].astype(F32)
    for rc in range(RL // RCH):
        r = (lax.broadcasted_iota(I32, (1, RCH), 1) + rc * RCH).astype(F32)
        member = (r >= lo) & (r < hi)
        base = jnp.sum(jnp.where(member, lo, 0.0), axis=0, keepdims=True)
        mb = member.astype(BF16)
        pw = jnp.where(_dot(posmt, mb) == r - base, _dot(wt, mb), 0.0).astype(BF16)
        o_ref[0] += _dot(pw, buf[rc * RCH:(rc + 1) * RCH, :])

    m = mod_ref[0, 0]
    o_ref[0] = h_ref[0] + m[5:6] * _rms(o_ref[0], gpost_ref[...])


def _combine(y, posm_t, wts_t, meta, fx, h2, mods, tabs, w_sgu, w_sd, g_post, n_lat):
    b, rows, d = fx.shape
    nt = rows // TM
    ntiles = b * nt
    nlt = n_lat // TM
    ne = posm_t.shape[1]
    tile = lambda t, *_: (t // nt, t % nt, 0)
    return pl.pallas_call(
        _combine_kernel,
        grid_spec=pltpu.PrefetchScalarGridSpec(
            num_scalar_prefetch=3,
            grid=(ntiles,),
            in_specs=[
                pl.BlockSpec(memory_space=pl.ANY),
                pl.BlockSpec((TM, ne), lambda t, *_: (t, 0)),
                pl.BlockSpec((TM, ne), lambda t, *_: (t, 0)),
                pl.BlockSpec((1,) + meta.shape[1:], lambda t, *_: (t, 0, 0)),
                pl.BlockSpec((1, TM, d), tile),
                pl.BlockSpec((1, TM, d), tile),
                pl.BlockSpec((1, 1, 6, d), lambda t, *_: (t // nt, jnp.where(t % nt < nlt, 1, 0), 0, 0)),
                pl.BlockSpec(w_sgu.shape, lambda t, *_: (0, 0), pipeline_mode=pl.Buffered(1)),
                pl.BlockSpec(w_sd.shape, lambda t, *_: (0, 0), pipeline_mode=pl.Buffered(1)),
                pl.BlockSpec(g_post.shape, lambda t, *_: (0, 0), pipeline_mode=pl.Buffered(1)),
            ],
            out_specs=pl.BlockSpec((1, TM, d), tile),
            scratch_shapes=[pltpu.VMEM((2, RL, d), BF16), pltpu.SemaphoreType.DMA((2,))],
        ),
        out_shape=jax.ShapeDtypeStruct((b, rows, d), F32),
        compiler_params=pltpu.CompilerParams(dimension_semantics=("arbitrary",), vmem_limit_bytes=VMEM_LIMIT),
        name="combine",
    )(tabs["n16"], tabs["loc"], tabs["seg"], y, posm_t, wts_t, meta, fx, h2, mods, w_sgu, w_sd, g_post)


def _rope_tables(n_lat, n_ctx, rot_dim, lane_slots):
    rows = n_lat // GRID_W
    pos_row = jnp.repeat(jnp.arange(rows, dtype=F32), GRID_W)
    pos_col = jnp.tile(jnp.arange(GRID_W, dtype=F32), rows)
    d_axis = rot_dim // 2
    inv_freq = ROPE_BASE ** (-jnp.arange(0, d_axis, 2, dtype=F32) / d_axis)
    ar = pos_row[:, None] * inv_freq
    ac = pos_col[:, None] * inv_freq
    ang = jnp.concatenate([ar, ar, ac, ac], axis=1)
    qtr = rot_dim // 4
    first = (jnp.arange(rot_dim) // qtr) % 2 == 0
    cos, sin = jnp.cos(ang), jnp.sin(ang)
    s1 = jnp.where(first, -sin, 0.0)
    s2 = jnp.where(first, 0.0, sin)
    lc = n_lat + n_ctx
    c_t = jnp.ones((lc, LANES), F32)
    s1_t = jnp.zeros((lc, LANES), F32)
    s2_t = jnp.zeros((lc, LANES), F32)
    for lo in lane_slots:
        c_t = c_t.at[:n_lat, lo:lo + rot_dim].set(cos)
        s1_t = s1_t.at[:n_lat, lo:lo + rot_dim].set(s1)
        s2_t = s2_t.at[:n_lat, lo:lo + rot_dim].set(s2)
    return jnp.stack([c_t, s1_t, s2_t])


def _pack_layer_weights(w_in, w_uq, w_ukv):
    d = w_in.shape[0]
    kv0 = 0
    kr0 = kv0 + MLA_KV_RANK
    kd0 = kr0 + MLA_ROPE
    vd0 = kd0 + 512
    q0 = vd0 + 512
    qd0 = q0 + MLA_Q_RANK
    p0 = qd0 + 512
    g0 = p0 + 512
    kr = jnp.zeros((d, LANES), w_in.dtype).at[:, MLA_NOPE:MLA_NOPE + MLA_ROPE].set(w_in[:, kr0:kd0])
    w1 = jnp.concatenate([w_in[:, kv0:kr0], kr, w_in[:, kd0:vd0], w_in[:, vd0:q0], w_in[:, q0:qd0],
                          w_in[:, qd0:p0], w_in[:, p0:g0], w_in[:, g0:]], axis=1).astype(BF16)
    ukv = w_ukv.reshape(MLA_KV_RANK, MLA_HEADS, MLA_NOPE + MLA_V)
    uk = jnp.pad(ukv[:, :, :MLA_NOPE], ((0, 0), (0, 0), (0, LANES - MLA_NOPE))).reshape(MLA_KV_RANK, -1)
    uv = ukv[:, :, MLA_NOPE:].reshape(MLA_KV_RANK, -1)
    w_ukv_p = jnp.concatenate([uk, uv], axis=1).astype(BF16)
    uq = w_uq.reshape(MLA_Q_RANK, MLA_HEADS, MLA_NOPE + MLA_ROPE)
    w_uq_p = jnp.pad(uq, ((0, 0), (0, 0), (0, LANES - MLA_NOPE - MLA_ROPE))).reshape(MLA_Q_RANK, -1).astype(BF16)
    return w1, w_ukv_p, w_uq_p


def _routing_tables(meta, totals, rmax):
    cpad, loc, gpos = meta[:, :, 0], meta[:, :, 1], meta[:, :, 2]
    tot = totals[:, 0]
    reg = (tot + EBLK - 1) // EBLK * EBLK
    reg_end = jnp.cumsum(reg)
    reg_start = reg_end - reg
    n_used = jnp.maximum(reg_end[-1:] // EBLK, 1).astype(I32)
    blk_start = jnp.arange(rmax // EBLK, dtype=I32) * EBLK
    blk_e = jnp.minimum(jnp.sum(reg_end[None, :] <= blk_start[:, None], axis=1), N_EXPERTS - 1).astype(I32)
    ntiles, ne = loc.shape
    lohi = jnp.zeros((ntiles, 8, LANES), F32).at[:, 0, :].set(float(RL)).at[:, 0, :ne].set(loc.astype(F32))
    lohi = lohi.at[:, 1, :ne].set((loc + cpad).astype(F32))
    return dict(
        n16=(cpad // ALIGN).astype(I32), loc=loc.astype(I32), seg=(reg_start[None, :] + gpos).astype(I32),
        tail_start=(reg_start + tot).astype(I32), tail_n16=((reg - tot) // ALIGN).astype(I32),
        n_used=n_used), blk_e, lohi


def kernel(x, c, ctx, c_ctx, w_mod, b_mod, g_pre_mix, g_post_mix, g_pre_ffn, g_post_ffn, w_in, g_cq, w_uq, g_ckv, w_ukv, w_pool, pool_scale, lam_q1, lam_k1, lam_q2, lam_k2, g_subln, w_b_mla, w_b_pool, w_b_diff, w_out, w_router, b_router, w_e_gate, w_e_up, w_e_down, w_s_gate, w_s_up, w_s_down):
    b, n_lat, d = x.shape
    n_ctx = ctx.shape[1]
    depth = w_mod.shape[0]
    assert n_ctx == TM and n_lat % TQ == 0 and n_lat % GRID_W == 0
    lc = n_lat + n_ctx
    row2 = lambda v: v.reshape(1, -1)

    rows = (b + 1 + 7) // 8 * 8
    cvec = jnp.zeros((rows, d), F32).at[:b].set(c).at[b].set(c_ctx)
    mod_all = _modulation(cvec, w_mod, b_mod).reshape(depth, rows, 6, d)

    tab_m = _rope_tables(n_lat, n_ctx, MLA_ROPE, (MLA_NOPE,))
    tab_d = _rope_tables(n_lat, n_ctx, DIFF_HALF, (0, DIFF_HALF))

    h = jnp.concatenate([x, ctx], axis=1)
    for l in range(depth):
        last = l == depth - 1
        lam_init = 0.8 - 0.6 * math.exp(-0.3 * l)
        mods = jnp.stack([jnp.broadcast_to(mod_all[l, b], (b, 6, d)), mod_all[l, :b]], axis=1)
        w1, w_ukv_p, w_uq_p = _pack_layer_weights(w_in[l], w_uq[l], w_ukv[l])

        qm, km, vm, qd, kd, vd, pool_in, gates = _project(
            h, mods, row2(g_pre_mix[l]), w1, row2(g_ckv[l]), w_ukv_p, row2(g_cq[l]), w_uq_p, tab_m, tab_d, n_lat)
        o_mla, o_mla_ctx = _attention(
            functools.partial(_mla_kernel, n_lat=n_lat, has_ctx=not last), "mla_attention", qm, km, vm, (),
            n_lat, not last, MLA_HEADS // 2, 2 * LANES, 2 * MLA_V, 2 * MLA_V)
        lam_p = jnp.stack([lam_q1[l], lam_k1[l], lam_q2[l], lam_k2[l]])
        o_diff, o_diff_ctx = _attention(
            functools.partial(_diff_kernel, n_lat=n_lat, has_ctx=not last, lam_init=lam_init), "diff_attention",
            qd, kd, vd, (lam_p, row2(g_subln[l])), n_lat, not last, DIFF_HEADS, LANES, DIFF_V, DIFF_V)
        segments = ((0, n_lat),) if last else ((0, n_lat), (n_lat, n_ctx))
        o_pool = _pool_mix(pool_in, w_pool[l].astype(BF16), row2(pool_scale[l]), segments)

        nt = (n_lat if last else lc) // TM
        wb = jnp.stack([w_b_mla[l], w_b_pool[l], w_b_diff[l]]).astype(BF16)
        h2, fx, logits_t = _merge(o_mla, o_mla_ctx, o_diff, o_diff_ctx, o_pool, gates, h, mods, wb,
                                  w_out[l].astype(BF16), row2(g_post_mix[l]), row2(g_pre_ffn[l]), w_router[l].T,
                                  n_lat, nt)

        posm, wts, meta, totals = _route(logits_t, b_router[l].reshape(-1, 1))
        rmax = meta.shape[0] * RL + N_EXPERTS * EBLK
        tabs, blk_e, lohi = _routing_tables(meta, totals, rmax)
        xs = _dispatch(fx, posm, lohi, tabs, rmax)
        w_gu = jnp.concatenate([w_e_gate[l], w_e_up[l]], axis=-1).astype(BF16)
        ys = _experts(xs, blk_e, tabs["n_used"], w_gu, w_e_down[l].astype(BF16))
        w_sgu = jnp.concatenate([w_s_gate[l], w_s_up[l]], axis=-1).astype(BF16)
        h = _combine(ys, posm.T, wts.T, meta, fx, h2, mods, tabs, w_sgu, w_s_down[l].astype(BF16),
                     row2(g_post_ffn[l]), n_lat)
    return h
```

```python
import functools
import math

import jax
import jax.numpy as jnp
from jax import lax
from jax.experimental import pallas as pl
from jax.experimental.pallas import tpu as pltpu

F32 = jnp.float32
BF16 = jnp.bfloat16
I32 = jnp.int32

EPS = 1e-6
ROPE_BASE = 10000.0
GRID_W = 64

MLA_HEADS = 8
MLA_NOPE = 64
MLA_ROPE = 32
MLA_V = 64
MLA_Q_RANK = 384
MLA_KV_RANK = 256
POOL_WINDOWS = (2, 4, 8, 16)
POOL_GROUP_W = 128
DIFF_HEADS = 4
DIFF_HALF = 64
DIFF_V = 128
N_EXPERTS = 64
TOP_K = 8
N_GROUPS = 8
TOPK_GROUPS = 4
EXPERT_FF = 256
ROUTED_SCALE = 2.5

LANES = 128
TM = 256
TQ = 1024
QSPLIT = 4
ALIGN = 16
EBLK = 512
RL = TM * TOP_K + N_EXPERTS * ALIGN
RCH = 512
VMEM_LIMIT = 56 * 1024 * 1024
LOG2E = 1.4426950408889634

C_CKV = 0
C_KR = C_CKV + MLA_KV_RANK
C_KD = C_KR + LANES
C_VD = C_KD + 512
C_CQ = C_VD + 512
C_QD = C_CQ + MLA_Q_RANK
C_POOL = C_QD + 512
C_GATE = C_POOL + 512


def _rms(x, g):
    ms = jnp.mean(x * x, axis=-1, keepdims=True)
    return x * lax.rsqrt(ms + EPS) * g


def _dot(a, b):
    return jnp.dot(a, b, preferred_element_type=F32)


def _dot_nt(a, b):
    return lax.dot_general(a, b, (((1,), (1,)), ((), ())), preferred_element_type=F32)


def _split_bf16(x):
    hi = x.astype(BF16)
    lo = (x - hi.astype(F32)).astype(BF16)
    return hi, lo


def _silu(x):
    return x * jax.nn.sigmoid(x)


def _rope(x, tab, quarter):
    w = x.shape[-1]
    rep = w // LANES
    c, s1, s2 = (jnp.tile(t, (1, rep)) if rep > 1 else t for t in tab)
    return x * c + pltpu.roll(x, w - quarter, 1) * s1 + pltpu.roll(x, quarter, 1) * s2


def _const_spec(shape):
    nd = len(shape)
    return pl.BlockSpec(shape, lambda *_: (0,) * nd, pipeline_mode=pl.Buffered(1))


def _mod_kernel(a_ref, w_ref, b_ref, o_ref):
    a = _silu(a_ref[...])
    ahi, alo = _split_bf16(a)
    whi, wlo = _split_bf16(w_ref[...])
    o_ref[...] = _dot(ahi, whi) + _dot(alo, whi) + _dot(ahi, wlo) + b_ref[...]


def _modulation(cvec, w_mod, b_mod):
    nl, d, d6 = w_mod.shape
    rows = cvec.shape[0]
    nc = d6 // d
    return pl.pallas_call(
        _mod_kernel,
        grid=(nl, nc),
        in_specs=[
            pl.BlockSpec((rows, d), lambda l, j: (0, 0)),
            pl.BlockSpec((None, d, d), lambda l, j: (l, 0, j)),
            pl.BlockSpec((None, 1, d), lambda l, j: (l, 0, j)),
        ],
        out_specs=pl.BlockSpec((None, rows, d), lambda l, j: (l, 0, j)),
        out_shape=jax.ShapeDtypeStruct((nl, rows, d6), F32),
        name="modulation",
    )(cvec, w_mod, b_mod.reshape(nl, 1, d6))


def _mod_spec(d, n_lat_tiles):
    return pl.BlockSpec((1, 1, 6, d), lambda bi, i: (bi, jnp.where(i < n_lat_tiles, 1, 0), 0, 0))


def _proj_kernel(h_ref, mod_ref, gpre_ref, w1_ref, gckv_ref, wukv_ref, gcq_ref, wuq_ref, tm_ref, td_ref,
                 qm_ref, km_ref, vm_ref, qd_ref, kd_ref, vd_ref, pool_ref, gate_ref):
    d = h_ref.shape[-1]
    m = mod_ref[0, 0]
    hx = (_rms(h_ref[0], gpre_ref[...]) * (1.0 + m[1:2]) + m[0:1]).astype(BF16)
    tabm = (tm_ref[0], tm_ref[1], tm_ref[2])
    tabd = (td_ref[0], td_ref[1], td_ref[2])

    ckv = _dot(hx, w1_ref[:, C_CKV:C_KR])
    kv = _dot(_rms(ckv, gckv_ref[...]).astype(BF16), wukv_ref[...])
    kr = _rope(_dot(hx, w1_ref[:, C_KR:C_KD]), tabm, MLA_ROPE // 4)
    nk = MLA_HEADS * LANES
    km_ref[0] = (kv[:, :nk] + jnp.tile(kr, (1, MLA_HEADS))).astype(BF16)
    vm_ref[0] = kv[:, nk:].astype(BF16)

    kd_ref[0] = _rope(_dot(hx, w1_ref[:, C_KD:C_VD]), tabd, DIFF_HALF // 4).astype(BF16)
    vd_ref[0] = _dot(hx, w1_ref[:, C_VD:C_CQ]).astype(BF16)

    cq = _dot(hx, w1_ref[:, C_CQ:C_QD])
    q = _dot(_rms(cq, gcq_ref[...]).astype(BF16), wuq_ref[...])
    qm_scale = LOG2E / math.sqrt(MLA_NOPE + MLA_ROPE)
    qm_ref[0] = (_rope(q, tabm, MLA_ROPE // 4) * qm_scale).astype(BF16)
    qd_scale = LOG2E / math.sqrt(DIFF_HALF)
    qd_ref[0] = (_rope(_dot(hx, w1_ref[:, C_QD:C_POOL]), tabd, DIFF_HALF // 4) * qd_scale).astype(BF16)
    pool_ref[0] = _dot(hx, w1_ref[:, C_POOL:C_GATE]).astype(BF16)
    for j in range(3):
        lo = C_GATE + j * d
        gate_ref[0, :, j * d:(j + 1) * d] = jax.nn.sigmoid(_dot(hx, w1_ref[:, lo:lo + d])).astype(BF16)


def _project(h, mods, g_pre, w1, g_ckv, w_ukv, g_cq, w_uq, tab_m, tab_d, n_lat):
    b, lc, d = h.shape
    nt = lc // TM
    tile = lambda w: pl.BlockSpec((1, TM, w), lambda bi, i: (bi, i, 0))
    widths = (MLA_HEADS * LANES, MLA_HEADS * LANES, MLA_HEADS * MLA_V, 512, 512, 512, 512, 3 * d)
    return pl.pallas_call(
        _proj_kernel,
        grid=(b, nt),
        in_specs=[
            tile(d),
            _mod_spec(d, n_lat // TM),
            _const_spec(g_pre.shape),
            _const_spec(w1.shape),
            _const_spec(g_ckv.shape),
            _const_spec(w_ukv.shape),
            _const_spec(g_cq.shape),
            _const_spec(w_uq.shape),
            pl.BlockSpec((3, TM, LANES), lambda bi, i: (0, i, 0)),
            pl.BlockSpec((3, TM, LANES), lambda bi, i: (0, i, 0)),
        ],
        out_specs=[tile(w) for w in widths],
        out_shape=[jax.ShapeDtypeStruct((b, lc, w), BF16) for w in widths],
        compiler_params=pltpu.CompilerParams(
            dimension_semantics=("parallel", "parallel"), vmem_limit_bytes=VMEM_LIMIT),
        name="project",
    )(h, mods, g_pre, w1, g_ckv, w_ukv, g_cq, w_uq, tab_m, tab_d)


def _exp_scores(q, k):
    s = _dot_nt(q, k)
    return jnp.exp2(s - jnp.max(s, axis=-1, keepdims=True))


def _attend_steps(attend, n_lat_steps, n_lat, lc, refs_lat, refs_ctx):
    i = pl.program_id(2)
    if refs_ctx is None:
        attend(*refs_lat, 0, lc)
    else:
        pl.when(i < n_lat_steps)(lambda: attend(*refs_lat, 0, lc))
        pl.when(i == n_lat_steps)(lambda: attend(*refs_ctx, n_lat, lc))


def _mla_kernel(*refs, n_lat, has_ctx):
    if has_ctx:
        ql_ref, qc_ref, k_ref, v_ref, ol_ref, oc_ref, vext_ref = refs
    else:
        ql_ref, k_ref, v_ref, ol_ref, vext_ref = refs
    lc = k_ref.shape[1]
    lane = lax.broadcasted_iota(I32, (1, LANES), 1)

    @pl.when(pl.program_id(2) == 0)
    def _():
        v = v_ref[0]
        one = jnp.ones_like(v)
        vext_ref[0] = jnp.where(lane < MLA_V, v, one)
        vext_ref[1] = jnp.where(lane < MLA_V, one, v)

    def attend(q_ref, o_ref, k0, k1):
        rows = q_ref.shape[1]
        step = rows // QSPLIT if rows >= QSPLIT * LANES else rows
        for r0 in range(0, rows, step):
            pv = []
            for j in range(2):
                e = _exp_scores(q_ref[0, r0:r0 + step, j * LANES:(j + 1) * LANES],
                                k_ref[0, k0:k1, j * LANES:(j + 1) * LANES])
                pv.append(_dot(e.astype(BF16), vext_ref[j, k0:k1, :]))
            o0 = pv[0] * (1.0 / pv[0][:, MLA_V:MLA_V + 1])
            o1 = pv[1] * (1.0 / pv[1][:, 0:1])
            o_ref[0, r0:r0 + step, :] = jnp.where(lane < MLA_V, o0, o1).astype(o_ref.dtype)

    _attend_steps(attend, n_lat // TQ, n_lat, lc, (ql_ref, ol_ref), (qc_ref, oc_ref) if has_ctx else None)


def _diff_kernel(*refs, n_lat, has_ctx, lam_init):
    if has_ctx:
        ql_ref, qc_ref, k_ref, v_ref, lam_ref, g_ref, ol_ref, oc_ref = refs
    else:
        ql_ref, k_ref, v_ref, lam_ref, g_ref, ol_ref = refs
    lc = k_ref.shape[1]
    lane = lax.broadcasted_iota(I32, (1, LANES), 1)
    lp = lam_ref[...]
    lam = (jnp.exp(jnp.sum(lp[0:1] * lp[1:2], axis=-1, keepdims=True))
           - jnp.exp(jnp.sum(lp[2:3] * lp[3:4], axis=-1, keepdims=True)) + lam_init)

    def attend(q_ref, o_ref, k0, k1):
        rows = q_ref.shape[1]
        step = rows // QSPLIT if rows >= QSPLIT * LANES else rows
        k = k_ref[0, k0:k1, :]
        v = v_ref[0, k0:k1, :]
        for r0 in range(0, rows, step):
            q = q_ref[0, r0:r0 + step, :]
            zero = jnp.zeros_like(q)
            e0 = _exp_scores(jnp.where(lane < DIFF_HALF, q, zero), k)
            e1 = _exp_scores(jnp.where(lane < DIFF_HALF, zero, q), k)
            o = (_dot(e0.astype(BF16), v) * (1.0 / jnp.sum(e0, axis=-1, keepdims=True))
                 - _dot(e1.astype(BF16), v) * (lam / jnp.sum(e1, axis=-1, keepdims=True)))
            o_ref[0, r0:r0 + step, :] = (_rms(o, g_ref[...]) * (1.0 - lam_init)).astype(o_ref.dtype)

    _attend_steps(attend, n_lat // TQ, n_lat, lc, (ql_ref, ol_ref), (qc_ref, oc_ref) if has_ctx else None)


def _attention(body, name, q, k, v, extras, scratch, n_lat, has_ctx, heads, wq, wv, wo):
    b, lc, _ = q.shape
    n_ctx = lc - n_lat
    nql = n_lat // TQ
    lat = lambda w: pl.BlockSpec((1, TQ, w), lambda bi, h, i: (bi, jnp.minimum(i, nql - 1), h))
    ctx_in = pl.BlockSpec((1, n_ctx, wq), lambda bi, h, i: (bi, n_lat // n_ctx, h))
    ctx_out = pl.BlockSpec((1, n_ctx, wo), lambda bi, h, i: (bi, 0, h))
    full = lambda w: pl.BlockSpec((1, lc, w), lambda bi, h, i: (bi, 0, h))
    small = [pl.BlockSpec(a.shape, lambda bi, h, i: (0, 0)) for a in extras]
    in_specs = [lat(wq)] + ([ctx_in] if has_ctx else []) + [full(wq), full(wv)] + small
    args = [q] + ([q] if has_ctx else []) + [k, v] + list(extras)
    out_specs = [lat(wo)] + ([ctx_out] if has_ctx else [])
    out_shape = [jax.ShapeDtypeStruct((b, n_lat, heads * wo), BF16)] + (
        [jax.ShapeDtypeStruct((b, n_ctx, heads * wo), BF16)] if has_ctx else [])
    outs = pl.pallas_call(
        body,
        grid=(b, heads, nql + (1 if has_ctx else 0)),
        in_specs=in_specs,
        out_specs=out_specs,
        out_shape=out_shape,
        scratch_shapes=scratch,
        compiler_params=pltpu.CompilerParams(
            dimension_semantics=("parallel", "parallel", "arbitrary"), vmem_limit_bytes=VMEM_LIMIT),
        name=name,
    )(*args)
    return outs if has_ctx else (outs[0], None)


POOL_HALO = 16


def _pool_kernel(u_ref, w_ref, s_ref, o_ref, pad_ref, *, segments):
    for s0, n in segments:
        t = lax.broadcasted_iota(I32, (n, 1), 0)
        for g, w in enumerate(POOL_WINDOWS):
            cols = slice(g * POOL_GROUP_W, (g + 1) * POOL_GROUP_W)
            xg = u_ref[0, s0:s0 + n, cols].astype(F32)
            zeros = jnp.zeros((POOL_HALO, POOL_GROUP_W), F32)
            pad_ref[0:POOL_HALO, :] = zeros
            pad_ref[POOL_HALO + n:2 * POOL_HALO + n, :] = zeros
            pad_ref[POOL_HALO:POOL_HALO + n, :] = xg
            acc = pad_ref[POOL_HALO - w // 2:POOL_HALO - w // 2 + n, :]
            for j in range(1 - w // 2, w // 2):
                acc = acc + pad_ref[POOL_HALO + j:POOL_HALO + j + n, :]
            cnt = (jnp.minimum(t + w // 2, n) - jnp.maximum(t - w // 2, 0)).astype(F32)
            pooled = (acc / cnt - xg).astype(BF16)
            o_ref[0, s0:s0 + n, cols] = (_dot(pooled, w_ref[g]) * s_ref[:, cols]).astype(o_ref.dtype)


def _pool_mix(pool_in, w_pool, pool_scale, segments):
    b, lc, pw = pool_in.shape
    nmax = max(n for _, n in segments)
    lo = sum(n for _, n in segments)
    return pl.pallas_call(
        functools.partial(_pool_kernel, segments=segments),
        grid=(b,),
        in_specs=[
            pl.BlockSpec((1, lc, pw), lambda bi: (bi, 0, 0)),
            pl.BlockSpec(w_pool.shape, lambda bi: (0, 0, 0)),
            pl.BlockSpec(pool_scale.shape, lambda bi: (0, 0)),
        ],
        out_specs=pl.BlockSpec((1, lo, pw), lambda bi: (bi, 0, 0)),
        out_shape=jax.ShapeDtypeStruct((b, lo, pw), BF16),
        scratch_shapes=[pltpu.VMEM((nmax + 2 * POOL_HALO, POOL_GROUP_W), F32)],
        compiler_params=pltpu.CompilerParams(dimension_semantics=("parallel",), vmem_limit_bytes=VMEM_LIMIT),
        name="pool_mix",
    )(pool_in, w_pool, pool_scale)


def _merge_kernel(*refs, n_lat_tiles, has_ctx):
    if has_ctx:
        (oml_ref, omc_ref, odl_ref, odc_ref, op_ref, gate_ref, h_ref, mod_ref, wb_ref, wo_ref, gpost_ref, gffn_ref,
         wr_ref, h2_ref, fx_ref, lg_ref) = refs
        is_ctx = pl.program_id(1) >= n_lat_tiles
        om = jnp.where(is_ctx, omc_ref[0], oml_ref[0])
        od = jnp.where(is_ctx, odc_ref[0], odl_ref[0])
    else:
        (oml_ref, odl_ref, op_ref, gate_ref, h_ref, mod_ref, wb_ref, wo_ref, gpost_ref, gffn_ref,
         wr_ref, h2_ref, fx_ref, lg_ref) = refs
        om = oml_ref[0]
        od = odl_ref[0]
    d = h_ref.shape[-1]
    m = mod_ref[0, 0]
    merged = (gate_ref[0, :, 0:d].astype(F32) * _dot(om, wb_ref[0])
              + gate_ref[0, :, d:2 * d].astype(F32) * _dot(op_ref[0], wb_ref[1])
              + gate_ref[0, :, 2 * d:3 * d].astype(F32) * _dot(od, wb_ref[2]))
    y = _dot(merged.astype(BF16), wo_ref[...])
    h2 = h_ref[0] + m[2:3] * _rms(y, gpost_ref[...])
    h2_ref[0] = h2
    fx = _rms(h2, gffn_ref[...]) * (1.0 + m[4:5]) + m[3:4]
    fhi, flo = _split_bf16(fx)
    fx_ref[0] = fhi
    whi, wlo = _split_bf16(wr_ref[...])
    lg_ref[...] = _dot_nt(whi, fhi) + _dot_nt(whi, flo) + _dot_nt(wlo, fhi)


def _merge(o_mla, o_mla_ctx, o_diff, o_diff_ctx, o_pool, gates, h, mods, wb, w_out, g_post, g_ffn, w_rt, n_lat, nt):
    b, _, d = h.shape
    ne = w_rt.shape[0]
    nlt = n_lat // TM
    has_ctx = o_mla_ctx is not None
    tile = lambda w: pl.BlockSpec((1, TM, w), lambda bi, i: (bi, i, 0))
    lat = lambda w: pl.BlockSpec((1, TM, w), lambda bi, i: (bi, jnp.minimum(i, nlt - 1), 0))
    ctx = lambda w: pl.BlockSpec((1, TM, w), lambda bi, i: (bi, 0, 0))
    wm, wd = o_mla.shape[-1], o_diff.shape[-1]
    if has_ctx:
        attn_specs = [lat(wm), ctx(wm), lat(wd), ctx(wd)]
        attn_args = [o_mla, o_mla_ctx, o_diff, o_diff_ctx]
    else:
        attn_specs = [tile(wm), tile(wd)]
        attn_args = [o_mla, o_diff]
    return pl.pallas_call(
        functools.partial(_merge_kernel, n_lat_tiles=nlt, has_ctx=has_ctx),
        grid=(b, nt),
        in_specs=attn_specs + [
            tile(o_pool.shape[-1]), tile(gates.shape[-1]), tile(d), _mod_spec(d, nlt),
            _const_spec(wb.shape), _const_spec(w_out.shape), _const_spec(g_post.shape), _const_spec(g_ffn.shape),
            _const_spec(w_rt.shape),
        ],
        out_specs=[tile(d), tile(d), pl.BlockSpec((ne, TM), lambda bi, i: (0, bi * nt + i))],
        out_shape=[jax.ShapeDtypeStruct((b, nt * TM, d), F32), jax.ShapeDtypeStruct((b, nt * TM, d), BF16),
                   jax.ShapeDtypeStruct((ne, b * nt * TM), F32)],
        compiler_params=pltpu.CompilerParams(
            dimension_semantics=("parallel", "parallel"), vmem_limit_bytes=VMEM_LIMIT),
        name="merge",
    )(*attn_args, o_pool, gates, h, mods, wb, w_out, g_post, g_ffn, w_rt)


def _route_kernel(lg_ref, br_ref, posm_ref, wts_ref, meta_ref, tot_ref, carry_ref):
    ne, tm = lg_ref.shape
    gsz = ne // N_GROUPS
    neg = -jnp.inf

    @pl.when(pl.program_id(0) == 0)
    def _():
        carry_ref[...] = jnp.zeros_like(carry_ref)

    scores = jax.nn.sigmoid(lg_ref[...])
    sel = scores + br_ref[...]

    jidx = lax.broadcasted_iota(I32, (gsz, tm), 0).astype(F32)
    gscore = []
    for g in range(N_GROUPS):
        v = sel[g * gsz:(g + 1) * gsz]
        m1 = jnp.max(v, axis=0, keepdims=True)
        first = jnp.min(jnp.where(v == m1, jidx, float(gsz)), axis=0, keepdims=True)
        m2 = jnp.max(jnp.where(jidx == first, neg, v), axis=0, keepdims=True)
        gscore.append(m1 + m2)
    masked = []
    for g in range(N_GROUPS):
        rank = jnp.zeros((1, tm), I32)
        for o in range(N_GROUPS):
            if o == g:
                continue
            ahead = gscore[o] > gscore[g]
            if o < g:
                ahead = ahead | (gscore[o] == gscore[g])
            rank = rank + ahead.astype(I32)
        masked.append(jnp.where(rank < TOPK_GROUPS, sel[g * gsz:(g + 1) * gsz], neg))
    cur = jnp.concatenate(masked, axis=0)

    eidx = lax.broadcasted_iota(I32, (ne, tm), 0).astype(F32)
    chosen = jnp.zeros((ne, tm), jnp.bool_)
    for _ in range(TOP_K):
        mx = jnp.max(cur, axis=0, keepdims=True)
        first = jnp.min(jnp.where(cur == mx, eidx, float(ne)), axis=0, keepdims=True)
        hit = eidx == first
        chosen = chosen | hit
        cur = jnp.where(hit, neg, cur)

    wsel = jnp.where(chosen, scores, 0.0)
    wts_ref[...] = wsel / jnp.sum(wsel, axis=0, keepdims=True) * ROUTED_SCALE

    cmask = chosen.astype(BF16)
    ti = lax.broadcasted_iota(I32, (tm, tm), 0)
    tj = lax.broadcasted_iota(I32, (tm, tm), 1)
    pos = _dot(cmask, (ti < tj).astype(BF16))
    posm_ref[...] = jnp.where(chosen, pos, -1.0)
    ei = lax.broadcasted_iota(I32, (ne, ne), 0)
    ej = lax.broadcasted_iota(I32, (ne, ne), 1)
    cnt = jnp.sum(chosen.astype(F32), axis=1, keepdims=True)
    cpad = jnp.floor((cnt + (ALIGN - 1)) * (1.0 / ALIGN)) * ALIGN
    cpad_b = jnp.broadcast_to(cpad, (ne, LANES))
    loc_b = _dot((ej < ei).astype(BF16), cpad_b.astype(BF16))

    gpos = carry_ref[...]
    lane = lax.broadcasted_iota(I32, (ne, LANES), 1)
    meta_ref[0] = jnp.where(lane == 0, cpad_b, jnp.where(lane == 1, loc_b, gpos)).astype(I32)
    carry_ref[...] = gpos + cpad_b
    tot_ref[...] = (gpos + cpad_b).astype(I32)


def _route(logits_t, b_router):
    ne, ntok = logits_t.shape
    ntiles = ntok // TM
    return pl.pallas_call(
        _route_kernel,
        grid=(ntiles,),
        in_specs=[pl.BlockSpec((ne, TM), lambda t: (0, t)), pl.BlockSpec((ne, 1), lambda t: (0, 0))],
        out_specs=[
            pl.BlockSpec((ne, TM), lambda t: (0, t)),
            pl.BlockSpec((ne, TM), lambda t: (0, t)),
            pl.BlockSpec((1, ne, LANES), lambda t: (t, 0, 0)),
            pl.BlockSpec((ne, LANES), lambda t: (0, 0)),
        ],
        out_shape=[
            jax.ShapeDtypeStruct((ne, ntok), F32),
            jax.ShapeDtypeStruct((ne, ntok), F32),
            jax.ShapeDtypeStruct((ntiles, ne, LANES), I32),
            jax.ShapeDtypeStruct((ne, LANES), I32),
        ],
        scratch_shapes=[pltpu.VMEM((ne, LANES), F32)],
        compiler_params=pltpu.CompilerParams(dimension_semantics=("arbitrary",)),
        name="route",
    )(logits_t, b_router)


WAIT_ROWS = 16 * ALIGN
START_UNROLL = 4


def _copy(local, remote, l0, r0, rows, sem, to_remote):
    lrows = local.at[pl.ds(l0, rows)]
    rrows = remote.at[pl.ds(r0, rows)]
    return pltpu.make_async_copy(lrows, rrows, sem) if to_remote else pltpu.make_async_copy(rrows, lrows, sem)


def _start_chunks(rem_ref, n, local, remote, sem, to_remote):
    def start(c):
        _copy(local, remote, pl.multiple_of(c * ALIGN, ALIGN), pl.multiple_of(rem_ref[0, 0, c], ALIGN), ALIGN,
              sem, to_remote).start()

    def per_group(g, carry):
        for u in range(START_UNROLL):
            start(g * START_UNROLL + u)
        return carry

    groups = n // START_UNROLL
    lax.fori_loop(0, groups, per_group, 0)
    lax.fori_loop(groups * START_UNROLL, n, lambda c, carry: (start(c), carry)[1], 0)


def _wait_chunks(n, local, remote, sem, to_remote):
    per = WAIT_ROWS // ALIGN
    lax.fori_loop(0, n // per, lambda i, c: (_copy(local, remote, 0, 0, WAIT_ROWS, sem, to_remote).wait(), c)[1], 0)
    lax.fori_loop(0, lax.rem(n, per), lambda i, c: (_copy(local, remote, 0, 0, ALIGN, sem, to_remote).wait(), c)[1], 0)


def _dispatch_kernel(n16_ref, tail_ref, tailn_ref, nused_ref, fx_ref, posm_ref, lohi_ref, rem_ref, xs_ref,
                     buf_ref, zero_ref, sem):
    t = pl.program_id(0)
    slot = lax.rem(t, 2)
    x = fx_ref[0]
    buf = buf_ref.at[slot]
    posm = posm_ref[...]
    m2 = jnp.concatenate([posm, jnp.zeros((LANES - posm.shape[0], posm.shape[1]), F32)], axis=0).astype(BF16)
    lo = lohi_ref[0, 0:1, :]
    hi = lohi_ref[0, 1:2, :]
    for rc in range(RL // RCH):
        r = (lax.broadcasted_iota(I32, (RCH, 1), 0) + rc * RCH).astype(F32)
        member = (r >= lo) & (r < hi)
        base = jnp.sum(jnp.where(member, lo, 0.0), axis=1, keepdims=True)
        onehot = (_dot(member.astype(BF16), m2) == r - base).astype(BF16)
        buf[rc * RCH:(rc + 1) * RCH, :] = _dot(onehot, x).astype(buf_ref.dtype)

    @pl.when(t > 0)
    def _():
        _wait_chunks(n16_ref[t - 1], buf_ref.at[1 - slot], xs_ref, sem.at[0], True)

    _start_chunks(rem_ref, n16_ref[t], buf, xs_ref, sem.at[0], True)

    def tail(act):
        def go(cp):
            if act == 0:
                cp.start()
            else:
                cp.wait()

        def per_expert(e, carry):
            s0 = tail_ref[e]

            def per_chunk(j, c):
                go(pltpu.make_async_copy(
                    zero_ref.at[pl.ds(0, ALIGN)],
                    xs_ref.at[pl.ds(pl.multiple_of(s0 + j * ALIGN, ALIGN), ALIGN)], sem.at[1]))
                return c

            lax.fori_loop(0, tailn_ref[e], per_chunk, 0)
            return carry

        lax.fori_loop(0, N_EXPERTS, per_expert, 0)

        def per_block(j, c):
            go(pltpu.make_async_copy(zero_ref, xs_ref.at[pl.ds(pl.multiple_of(j * EBLK, EBLK), EBLK)], sem.at[1]))
            return c

        lax.fori_loop(nused_ref[0], xs_ref.shape[0] // EBLK, per_block, 0)

    @pl.when(t == pl.num_programs(0) - 1)
    def _():
        zero_ref[...] = jnp.zeros_like(zero_ref)
        tail(0)
        _wait_chunks(n16_ref[t], buf, xs_ref, sem.at[0], True)
        tail(1)


def _dispatch(fx, posm, lohi, tabs, rmax):
    b, rows, d = fx.shape
    nt = rows // TM
    ntiles = b * nt
    rem = tabs["rem"]
    return pl.pallas_call(
        _dispatch_kernel,
        grid_spec=pltpu.PrefetchScalarGridSpec(
            num_scalar_prefetch=4,
            grid=(ntiles,),
            in_specs=[
                pl.BlockSpec((1, TM, d), lambda t, *_: (t // nt, t % nt, 0)),
                pl.BlockSpec((posm.shape[0], TM), lambda t, *_: (0, t)),
                pl.BlockSpec((1,) + lohi.shape[1:], lambda t, *_: (t, 0, 0)),
                pl.BlockSpec((1,) + rem.shape[1:], lambda t, *_: (t, 0, 0), memory_space=pltpu.SMEM),
            ],
            out_specs=pl.BlockSpec(memory_space=pl.ANY),
            scratch_shapes=[pltpu.VMEM((2, RL, d), BF16), pltpu.VMEM((EBLK, d), BF16),
                            pltpu.SemaphoreType.DMA((2,))],
        ),
        out_shape=jax.ShapeDtypeStruct((rmax, d), BF16),
        compiler_params=pltpu.CompilerParams(dimension_semantics=("arbitrary",), vmem_limit_bytes=VMEM_LIMIT),
        name="dispatch",
    )(tabs["n16"], tabs["tail_start"], tabs["tail_n16"], tabs["n_used"], fx, posm, lohi, rem)


def _expert_kernel(blk_ref, nused_ref, x_ref, wgu_ref, wd_ref, y_ref):
    @pl.when(pl.program_id(0) < nused_ref[0])
    def _():
        gu = _dot(x_ref[...], wgu_ref[0])
        hmid = (_silu(gu[:, :EXPERT_FF]) * gu[:, EXPERT_FF:]).astype(BF16)
        y_ref[...] = _dot(hmid, wd_ref[0]).astype(y_ref.dtype)


def _experts(xs, blk_e, n_used, w_gu, w_d):
    rmax, d = xs.shape
    nblk = rmax // EBLK
    row = lambda i, blk, nu: (jnp.minimum(i, nu[0] - 1), 0)
    wsel = lambda i, blk, nu: (blk[jnp.minimum(i, nu[0] - 1)], 0, 0)
    return pl.pallas_call(
        _expert_kernel,
        grid_spec=pltpu.PrefetchScalarGridSpec(
            num_scalar_prefetch=2,
            grid=(nblk,),
            in_specs=[
                pl.BlockSpec((EBLK, d), row),
                pl.BlockSpec((1, d, 2 * EXPERT_FF), wsel),
                pl.BlockSpec((1, EXPERT_FF, d), wsel),
            ],
            out_specs=pl.BlockSpec((EBLK, d), row),
        ),
        out_shape=jax.ShapeDtypeStruct((rmax, d), BF16),
        input_output_aliases={2: 0},
        compiler_params=pltpu.CompilerParams(dimension_semantics=("arbitrary",), vmem_limit_bytes=VMEM_LIMIT),
        name="experts",
    )(blk_e, n_used, xs, w_gu, w_d)


def _combine_kernel(n16_ref, y_ref, posmt_ref, wt_ref, meta_ref, fx_ref, h_ref, mod_ref, wsgu_ref,
                    wsd_ref, gpost_ref, rem_ref, remn_ref, o_ref, buf_ref, sem):
    t = pl.program_id(0)
    slot = lax.rem(t, 2)
    buf = buf_ref.at[slot]

    @pl.when(t == 0)
    def _():
        buf_ref[...] = jnp.zeros_like(buf_ref)
        _start_chunks(rem_ref, n16_ref[t], buf, y_ref, sem.at[slot], False)

    @pl.when(t + 1 < pl.num_programs(0))
    def _():
        _start_chunks(remn_ref, n16_ref[t + 1], buf_ref.at[1 - slot], y_ref, sem.at[1 - slot], False)

    x = fx_ref[0]
    gu = _dot(x, wsgu_ref[...])
    ff = wsd_ref.shape[0]
    o_ref[0] = _dot((_silu(gu[:, :ff]) * gu[:, ff:]).astype(BF16), wsd_ref[...])

    _wait_chunks(n16_ref[t], buf, y_ref, sem.at[slot], False)

    posmt = posmt_ref[...].astype(BF16)
    wt = wt_ref[...].astype(BF16)
    lo = meta_ref[0, :, 1:2].astype(F32)
    hi = lo + meta_ref[0, :, 0:1].astype(F32)
    for rc in range(RL // RCH):
        r = (lax.broadcasted_iota(I32, (1, RCH), 1) + rc * RCH).astype(F32)
        member = (r >= lo) & (r < hi)
        base = jnp.sum(jnp.where(member, lo, 0.0), axis=0, keepdims=True)
        mb = member.astype(BF16)
        pw = jnp.where(_dot(posmt, mb) == r - base, _dot(wt, mb), 0.0).astype(BF16)
        o_ref[0] += _dot(pw, buf[rc * RCH:(rc + 1) * RCH, :])

    m = mod_ref[0, 0]
    o_ref[0] = h_ref[0] + m[5:6] * _rms(o_ref[0], gpost_ref[...])


def _combine(y, posm_t, wts_t, meta, fx, h2, mods, tabs, w_sgu, w_sd, g_post, n_lat):
    b, rows, d = fx.shape
    nt = rows // TM
    ntiles = b * nt
    nlt = n_lat // TM
    ne = posm_t.shape[1]
    rem = tabs["rem"]
    tile = lambda t, *_: (t // nt, t % nt, 0)
    return pl.pallas_call(
        _combine_kernel,
        grid_spec=pltpu.PrefetchScalarGridSpec(
            num_scalar_prefetch=1,
            grid=(ntiles,),
            in_specs=[
                pl.BlockSpec(memory_space=pl.ANY),
                pl.BlockSpec((TM, ne), lambda t, *_: (t, 0)),
                pl.BlockSpec((TM, ne), lambda t, *_: (t, 0)),
                pl.BlockSpec((1,) + meta.shape[1:], lambda t, *_: (t, 0, 0)),
                pl.BlockSpec((1, TM, d), tile),
                pl.BlockSpec((1, TM, d), tile),
                pl.BlockSpec((1, 1, 6, d), lambda t, *_: (t // nt, jnp.where(t % nt < nlt, 1, 0), 0, 0)),
                pl.BlockSpec(w_sgu.shape, lambda t, *_: (0, 0), pipeline_mode=pl.Buffered(1)),
                pl.BlockSpec(w_sd.shape, lambda t, *_: (0, 0), pipeline_mode=pl.Buffered(1)),
                pl.BlockSpec(g_post.shape, lambda t, *_: (0, 0), pipeline_mode=pl.Buffered(1)),
                pl.BlockSpec((1,) + rem.shape[1:], lambda t, *_: (t, 0, 0), memory_space=pltpu.SMEM),
                pl.BlockSpec((1,) + rem.shape[1:], lambda t, *_: (jnp.minimum(t + 1, ntiles - 1), 0, 0),
                             memory_space=pltpu.SMEM),
            ],
            out_specs=pl.BlockSpec((1, TM, d), tile),
            scratch_shapes=[pltpu.VMEM((2, RL, d), BF16), pltpu.SemaphoreType.DMA((2,))],
        ),
        out_shape=jax.ShapeDtypeStruct((b, rows, d), F32),
        compiler_params=pltpu.CompilerParams(dimension_semantics=("arbitrary",), vmem_limit_bytes=VMEM_LIMIT),
        name="combine",
    )(tabs["n16"], y, posm_t, wts_t, meta, fx, h2, mods, w_sgu, w_sd, g_post, rem, rem)


def _rope_tables(n_lat, n_ctx, rot_dim, lane_slots):
    rows = n_lat // GRID_W
    pos_row = jnp.repeat(jnp.arange(rows, dtype=F32), GRID_W)
    pos_col = jnp.tile(jnp.arange(GRID_W, dtype=F32), rows)
    d_axis = rot_dim // 2
    inv_freq = ROPE_BASE ** (-jnp.arange(0, d_axis, 2, dtype=F32) / d_axis)
    ar = pos_row[:, None] * inv_freq
    ac = pos_col[:, None] * inv_freq
    ang = jnp.concatenate([ar, ar, ac, ac], axis=1)
    qtr = rot_dim // 4
    first = (jnp.arange(rot_dim) // qtr) % 2 == 0
    cos, sin = jnp.cos(ang), jnp.sin(ang)
    s1 = jnp.where(first, -sin, 0.0)
    s2 = jnp.where(first, 0.0, sin)
    lc = n_lat + n_ctx
    c_t = jnp.ones((lc, LANES), F32)
    s1_t = jnp.zeros((lc, LANES), F32)
    s2_t = jnp.zeros((lc, LANES), F32)
    for lo in lane_slots:
        c_t = c_t.at[:n_lat, lo:lo + rot_dim].set(cos)
        s1_t = s1_t.at[:n_lat, lo:lo + rot_dim].set(s1)
        s2_t = s2_t.at[:n_lat, lo:lo + rot_dim].set(s2)
    return jnp.stack([c_t, s1_t, s2_t])


def _pack_layer_weights(w_in, w_uq, w_ukv):
    d = w_in.shape[0]
    kv0 = 0
    kr0 = kv0 + MLA_KV_RANK
    kd0 = kr0 + MLA_ROPE
    vd0 = kd0 + 512
    q0 = vd0 + 512
    qd0 = q0 + MLA_Q_RANK
    p0 = qd0 + 512
    g0 = p0 + 512
    kr = jnp.zeros((d, LANES), w_in.dtype).at[:, MLA_NOPE:MLA_NOPE + MLA_ROPE].set(w_in[:, kr0:kd0])
    w1 = jnp.concatenate([w_in[:, kv0:kr0], kr, w_in[:, kd0:vd0], w_in[:, vd0:q0], w_in[:, q0:qd0],
                          w_in[:, qd0:p0], w_in[:, p0:g0], w_in[:, g0:]], axis=1).astype(BF16)
    ukv = w_ukv.reshape(MLA_KV_RANK, MLA_HEADS, MLA_NOPE + MLA_V)
    uk = jnp.pad(ukv[:, :, :MLA_NOPE], ((0, 0), (0, 0), (0, LANES - MLA_NOPE))).reshape(MLA_KV_RANK, -1)
    uv = ukv[:, :, MLA_NOPE:].reshape(MLA_KV_RANK, -1)
    w_ukv_p = jnp.concatenate([uk, uv], axis=1).astype(BF16)
    uq = w_uq.reshape(MLA_Q_RANK, MLA_HEADS, MLA_NOPE + MLA_ROPE)
    w_uq_p = jnp.pad(uq, ((0, 0), (0, 0), (0, LANES - MLA_NOPE - MLA_ROPE))).reshape(MLA_Q_RANK, -1).astype(BF16)
    return w1, w_ukv_p, w_uq_p


def _routing_tables(meta, totals, rmax):
    cpad, loc, gpos = meta[:, :, 0], meta[:, :, 1], meta[:, :, 2]
    tot = totals[:, 0]
    reg = (tot + EBLK - 1) // EBLK * EBLK
    reg_end = jnp.cumsum(reg)
    reg_start = reg_end - reg
    n_used = jnp.maximum(reg_end[-1:] // EBLK, 1).astype(I32)
    blk_start = jnp.arange(rmax // EBLK, dtype=I32) * EBLK
    blk_e = jnp.minimum(jnp.sum(reg_end[None, :] <= blk_start[:, None], axis=1), N_EXPERTS - 1).astype(I32)
    ntiles, ne = loc.shape
    lohi = jnp.zeros((ntiles, 8, LANES), F32).at[:, 0, :].set(float(RL)).at[:, 0, :ne].set(loc.astype(F32))
    lohi = lohi.at[:, 1, :ne].set((loc + cpad).astype(F32))
    chunk = jnp.arange(RL // ALIGN, dtype=I32)
    seg_end = (loc + cpad) // ALIGN
    e_of_chunk = jnp.minimum(jnp.sum(seg_end[:, None, :] <= chunk[None, :, None], axis=-1), ne - 1)
    shift = jnp.take_along_axis(reg_start[None, :] + gpos - loc, e_of_chunk, axis=1)
    rem = (shift + chunk[None, :] * ALIGN).astype(I32)[:, None, :]
    return dict(
        n16=seg_end[:, -1].astype(I32), rem=rem,
        tail_start=(reg_start + tot).astype(I32), tail_n16=((reg - tot) // ALIGN).astype(I32),
        n_used=n_used), blk_e, lohi


def kernel(x, c, ctx, c_ctx, w_mod, b_mod, g_pre_mix, g_post_mix, g_pre_ffn, g_post_ffn, w_in, g_cq, w_uq, g_ckv, w_ukv, w_pool, pool_scale, lam_q1, lam_k1, lam_q2, lam_k2, g_subln, w_b_mla, w_b_pool, w_b_diff, w_out, w_router, b_router, w_e_gate, w_e_up, w_e_down, w_s_gate, w_s_up, w_s_down):
    b, n_lat, d = x.shape
    n_ctx = ctx.shape[1]
    depth = w_mod.shape[0]
    assert n_ctx == TM and n_lat % TQ == 0 and n_lat % GRID_W == 0
    lc = n_lat + n_ctx
    row2 = lambda v: v.reshape(1, -1)

    rows = (b + 1 + 7) // 8 * 8
    cvec = jnp.zeros((rows, d), F32).at[:b].set(c).at[b].set(c_ctx)
    mod_all = _modulation(cvec, w_mod, b_mod).reshape(depth, rows, 6, d)

    tab_m = _rope_tables(n_lat, n_ctx, MLA_ROPE, (MLA_NOPE,))
    tab_d = _rope_tables(n_lat, n_ctx, DIFF_HALF, (0, DIFF_HALF))

    h = jnp.concatenate([x, ctx], axis=1)
    for l in range(depth):
        last = l == depth - 1
        lam_init = 0.8 - 0.6 * math.exp(-0.3 * l)
        mods = jnp.stack([jnp.broadcast_to(mod_all[l, b], (b, 6, d)), mod_all[l, :b]], axis=1)
        w1, w_ukv_p, w_uq_p = _pack_layer_weights(w_in[l], w_uq[l], w_ukv[l])

        qm, km, vm, qd, kd, vd, pool_in, gates = _project(
            h, mods, row2(g_pre_mix[l]), w1, row2(g_ckv[l]), w_ukv_p, row2(g_cq[l]), w_uq_p, tab_m, tab_d, n_lat)
        o_mla, o_mla_ctx = _attention(
            functools.partial(_mla_kernel, n_lat=n_lat, has_ctx=not last), "mla_attention", qm, km, vm, (),
            [pltpu.VMEM((2, lc, 2 * MLA_V), BF16)], n_lat, not last, MLA_HEADS // 2, 2 * LANES, 2 * MLA_V, 2 * MLA_V)
        lam_p = jnp.stack([lam_q1[l], lam_k1[l], lam_q2[l], lam_k2[l]])
        o_diff, o_diff_ctx = _attention(
            functools.partial(_diff_kernel, n_lat=n_lat, has_ctx=not last, lam_init=lam_init), "diff_attention",
            qd, kd, vd, (lam_p, row2(g_subln[l])), [], n_lat, not last, DIFF_HEADS, LANES, DIFF_V, DIFF_V)
        segments = ((0, n_lat),) if last else ((0, n_lat), (n_lat, n_ctx))
        o_pool = _pool_mix(pool_in, w_pool[l].astype(BF16), row2(pool_scale[l]), segments)

        nt = (n_lat if last else lc) // TM
        wb = jnp.stack([w_b_mla[l], w_b_pool[l], w_b_diff[l]]).astype(BF16)
        h2, fx, logits_t = _merge(o_mla, o_mla_ctx, o_diff, o_diff_ctx, o_pool, gates, h, mods, wb,
                                  w_out[l].astype(BF16), row2(g_post_mix[l]), row2(g_pre_ffn[l]), w_router[l].T,
                                  n_lat, nt)

        posm, wts, meta, totals = _route(logits_t, b_router[l].reshape(-1, 1))
        rmax = meta.shape[0] * RL + N_EXPERTS * EBLK
        tabs, blk_e, lohi = _routing_tables(meta, totals, rmax)
        xs = _dispatch(fx, posm, lohi, tabs, rmax)
        w_gu = jnp.concatenate([w_e_gate[l], w_e_up[l]], axis=-1).astype(BF16)
        ys = _experts(xs, blk_e, tabs["n_used"], w_gu, w_e_down[l].astype(BF16))
        w_sgu = jnp.concatenate([w_s_gate[l], w_s_up[l]], axis=-1).astype(BF16)
        h = _combine(ys, posm.T, wts.T, meta, fx, h2, mods, tabs, w_sgu, w_s_down[l].astype(BF16),
                     row2(g_post_ffn[l]), n_lat)
    return h
```

```python
import functools
import math

import jax
import jax.numpy as jnp
from jax import lax
from jax.experimental import pallas as pl
from jax.experimental.pallas import tpu as pltpu

F32 = jnp.float32
BF16 = jnp.bfloat16
I32 = jnp.int32

EPS = 1e-6
ROPE_BASE = 10000.0
GRID_W = 64

MLA_HEADS = 8
MLA_NOPE = 64
MLA_ROPE = 32
MLA_V = 64
MLA_Q_RANK = 384
MLA_KV_RANK = 256
POOL_WINDOWS = (2, 4, 8, 16)
POOL_GROUP_W = 128
DIFF_HEADS = 4
DIFF_HALF = 64
DIFF_V = 128
N_EXPERTS = 64
TOP_K = 8
N_GROUPS = 8
TOPK_GROUPS = 4
EXPERT_FF = 256
ROUTED_SCALE = 2.5

LANES = 128
TM = 256
TQ = 2048
QSPLIT = 8
ALIGN = 16
EBLK = 512
RL = TM * TOP_K + N_EXPERTS * ALIGN
RCH = 512
VMEM_LIMIT = 56 * 1024 * 1024
LOG2E = 1.4426950408889634

C_CKV = 0
C_KR = C_CKV + MLA_KV_RANK
C_KD = C_KR + LANES
C_VD = C_KD + 512
C_CQ = C_VD + 512
C_QD = C_CQ + MLA_Q_RANK
C_POOL = C_QD + 512
C_GATE = C_POOL + 512


def _rms(x, g):
    ms = jnp.mean(x * x, axis=-1, keepdims=True)
    return x * lax.rsqrt(ms + EPS) * g


def _dot(a, b):
    return jnp.dot(a, b, preferred_element_type=F32)


def _dot_nt(a, b):
    return lax.dot_general(a, b, (((1,), (1,)), ((), ())), preferred_element_type=F32)


def _split_bf16(x):
    hi = x.astype(BF16)
    lo = (x - hi.astype(F32)).astype(BF16)
    return hi, lo


def _silu(x):
    return x * jax.nn.sigmoid(x)


def _rope(x, tab, quarter):
    w = x.shape[-1]
    rep = w // LANES
    c, s1, s2 = (jnp.tile(t, (1, rep)) if rep > 1 else t for t in tab)
    return x * c + pltpu.roll(x, w - quarter, 1) * s1 + pltpu.roll(x, quarter, 1) * s2


def _const_spec(shape):
    nd = len(shape)
    return pl.BlockSpec(shape, lambda *_: (0,) * nd, pipeline_mode=pl.Buffered(1))


def _mod_kernel(a_ref, w_ref, b_ref, o_ref):
    a = _silu(a_ref[...])
    ahi, alo = _split_bf16(a)
    whi, wlo = _split_bf16(w_ref[...])
    o_ref[...] = _dot(ahi, whi) + _dot(alo, whi) + _dot(ahi, wlo) + b_ref[...]


def _modulation(cvec, w_mod, b_mod):
    nl, d, d6 = w_mod.shape
    rows = cvec.shape[0]
    nc = d6 // d
    return pl.pallas_call(
        _mod_kernel,
        grid=(nl, nc),
        in_specs=[
            pl.BlockSpec((rows, d), lambda l, j: (0, 0)),
            pl.BlockSpec((None, d, d), lambda l, j: (l, 0, j)),
            pl.BlockSpec((None, 1, d), lambda l, j: (l, 0, j)),
        ],
        out_specs=pl.BlockSpec((None, rows, d), lambda l, j: (l, 0, j)),
        out_shape=jax.ShapeDtypeStruct((nl, rows, d6), F32),
        name="modulation",
    )(cvec, w_mod, b_mod.reshape(nl, 1, d6))


def _mod_spec(d, n_lat_tiles):
    return pl.BlockSpec((1, 1, 6, d), lambda bi, i: (bi, jnp.where(i < n_lat_tiles, 1, 0), 0, 0))


def _proj_kernel(h_ref, mod_ref, gpre_ref, w1_ref, gckv_ref, wukv_ref, gcq_ref, wuq_ref, tm_ref, td_ref,
                 qm_ref, km_ref, vm_ref, qd_ref, kd_ref, vd_ref, pool_ref, gate_ref):
    d = h_ref.shape[-1]
    m = mod_ref[0, 0]
    hx = (_rms(h_ref[0], gpre_ref[...]) * (1.0 + m[1:2]) + m[0:1]).astype(BF16)
    tabm = (tm_ref[0], tm_ref[1], tm_ref[2])
    tabd = (td_ref[0], td_ref[1], td_ref[2])

    ckv = _dot(hx, w1_ref[:, C_CKV:C_KR])
    kv = _dot(_rms(ckv, gckv_ref[...]).astype(BF16), wukv_ref[...])
    kr = _rope(_dot(hx, w1_ref[:, C_KR:C_KD]), tabm, MLA_ROPE // 4)
    nk = MLA_HEADS * LANES
    km_ref[0] = (kv[:, :nk] + jnp.tile(kr, (1, MLA_HEADS))).astype(BF16)
    vm_ref[0] = kv[:, nk:].astype(BF16)

    kd_ref[0] = _rope(_dot(hx, w1_ref[:, C_KD:C_VD]), tabd, DIFF_HALF // 4).astype(BF16)
    vd_ref[0] = _dot(hx, w1_ref[:, C_VD:C_CQ]).astype(BF16)

    cq = _dot(hx, w1_ref[:, C_CQ:C_QD])
    q = _dot(_rms(cq, gcq_ref[...]).astype(BF16), wuq_ref[...])
    qm_scale = LOG2E / math.sqrt(MLA_NOPE + MLA_ROPE)
    qm_ref[0] = (_rope(q, tabm, MLA_ROPE // 4) * qm_scale).astype(BF16)
    qd_scale = LOG2E / math.sqrt(DIFF_HALF)
    qd_ref[0] = (_rope(_dot(hx, w1_ref[:, C_QD:C_POOL]), tabd, DIFF_HALF // 4) * qd_scale).astype(BF16)
    pool_ref[0] = _dot(hx, w1_ref[:, C_POOL:C_GATE]).astype(BF16)
    for j in range(3):
        lo = C_GATE + j * d
        gate_ref[0, :, j * d:(j + 1) * d] = jax.nn.sigmoid(_dot(hx, w1_ref[:, lo:lo + d])).astype(BF16)


def _project(h, mods, g_pre, w1, g_ckv, w_ukv, g_cq, w_uq, tab_m, tab_d, n_lat):
    b, lc, d = h.shape
    nt = lc // TM
    tile = lambda w: pl.BlockSpec((1, TM, w), lambda bi, i: (bi, i, 0))
    widths = (MLA_HEADS * LANES, MLA_HEADS * LANES, MLA_HEADS * MLA_V, 512, 512, 512, 512, 3 * d)
    return pl.pallas_call(
        _proj_kernel,
        grid=(b, nt),
        in_specs=[
            tile(d),
            _mod_spec(d, n_lat // TM),
            _const_spec(g_pre.shape),
            _const_spec(w1.shape),
            _const_spec(g_ckv.shape),
            _const_spec(w_ukv.shape),
            _const_spec(g_cq.shape),
            _const_spec(w_uq.shape),
            pl.BlockSpec((3, TM, LANES), lambda bi, i: (0, i, 0)),
            pl.BlockSpec((3, TM, LANES), lambda bi, i: (0, i, 0)),
        ],
        out_specs=[tile(w) for w in widths],
        out_shape=[jax.ShapeDtypeStruct((b, lc, w), BF16) for w in widths],
        compiler_params=pltpu.CompilerParams(
            dimension_semantics=("parallel", "parallel"), vmem_limit_bytes=VMEM_LIMIT),
        name="project",
    )(h, mods, g_pre, w1, g_ckv, w_ukv, g_cq, w_uq, tab_m, tab_d)


def _exp_scores(q, k):
    s = _dot_nt(q, k)
    return jnp.exp2(s - jnp.max(s, axis=-1, keepdims=True))


def _attend_steps(attend, n_lat_steps, n_lat, lc, refs_lat, refs_ctx):
    i = pl.program_id(2)
    if refs_ctx is None:
        attend(*refs_lat, 0, lc)
    else:
        pl.when(i < n_lat_steps)(lambda: attend(*refs_lat, 0, lc))
        pl.when(i == n_lat_steps)(lambda: attend(*refs_ctx, n_lat, lc))


def _mla_kernel(*refs, n_lat, has_ctx):
    if has_ctx:
        ql_ref, qc_ref, k_ref, v_ref, ol_ref, oc_ref, vext_ref = refs
    else:
        ql_ref, k_ref, v_ref, ol_ref, vext_ref = refs
    lc = k_ref.shape[1]
    lane = lax.broadcasted_iota(I32, (1, LANES), 1)

    @pl.when(pl.program_id(2) == 0)
    def _():
        v = v_ref[0]
        one = jnp.ones_like(v)
        vext_ref[0] = jnp.where(lane < MLA_V, v, one)
        vext_ref[1] = jnp.where(lane < MLA_V, one, v)

    def attend(q_ref, o_ref, k0, k1):
        rows = q_ref.shape[1]
        step = rows // QSPLIT if rows >= QSPLIT * LANES else rows
        for r0 in range(0, rows, step):
            pv = []
            for j in range(2):
                e = _exp_scores(q_ref[0, r0:r0 + step, j * LANES:(j + 1) * LANES],
                                k_ref[0, k0:k1, j * LANES:(j + 1) * LANES])
                pv.append(_dot(e.astype(BF16), vext_ref[j, k0:k1, :]))
            o0 = pv[0] * (1.0 / pv[0][:, MLA_V:MLA_V + 1])
            o1 = pv[1] * (1.0 / pv[1][:, 0:1])
            o_ref[0, r0:r0 + step, :] = jnp.where(lane < MLA_V, o0, o1).astype(o_ref.dtype)

    _attend_steps(attend, n_lat // TQ, n_lat, lc, (ql_ref, ol_ref), (qc_ref, oc_ref) if has_ctx else None)


def _diff_kernel(*refs, n_lat, has_ctx, lam_init):
    if has_ctx:
        ql_ref, qc_ref, k_ref, v_ref, lam_ref, g_ref, ol_ref, oc_ref = refs
    else:
        ql_ref, k_ref, v_ref, lam_ref, g_ref, ol_ref = refs
    lc = k_ref.shape[1]
    lane = lax.broadcasted_iota(I32, (1, LANES), 1)
    lp = lam_ref[...]
    lam = (jnp.exp(jnp.sum(lp[0:1] * lp[1:2], axis=-1, keepdims=True))
           - jnp.exp(jnp.sum(lp[2:3] * lp[3:4], axis=-1, keepdims=True)) + lam_init)

    def attend(q_ref, o_ref, k0, k1):
        rows = q_ref.shape[1]
        step = rows // QSPLIT if rows >= QSPLIT * LANES else rows
        k = k_ref[0, k0:k1, :]
        v = v_ref[0, k0:k1, :]
        for r0 in range(0, rows, step):
            q = q_ref[0, r0:r0 + step, :]
            zero = jnp.zeros_like(q)
            e0 = _exp_scores(jnp.where(lane < DIFF_HALF, q, zero), k)
            e1 = _exp_scores(jnp.where(lane < DIFF_HALF, zero, q), k)
            o = (_dot(e0.astype(BF16), v) * (1.0 / jnp.sum(e0, axis=-1, keepdims=True))
                 - _dot(e1.astype(BF16), v) * (lam / jnp.sum(e1, axis=-1, keepdims=True)))
            o_ref[0, r0:r0 + step, :] = (_rms(o, g_ref[...]) * (1.0 - lam_init)).astype(o_ref.dtype)

    _attend_steps(attend, n_lat // TQ, n_lat, lc, (ql_ref, ol_ref), (qc_ref, oc_ref) if has_ctx else None)


def _attention(body, name, q, k, v, extras, scratch, n_lat, has_ctx, heads, wq, wv, wo):
    b, lc, _ = q.shape
    n_ctx = lc - n_lat
    nql = n_lat // TQ
    lat = lambda w: pl.BlockSpec((1, TQ, w), lambda bi, h, i: (bi, jnp.minimum(i, nql - 1), h))
    ctx_in = pl.BlockSpec((1, n_ctx, wq), lambda bi, h, i: (bi, n_lat // n_ctx, h))
    ctx_out = pl.BlockSpec((1, n_ctx, wo), lambda bi, h, i: (bi, 0, h))
    full = lambda w: pl.BlockSpec((1, lc, w), lambda bi, h, i: (bi, 0, h))
    small = [pl.BlockSpec(a.shape, lambda bi, h, i: (0, 0)) for a in extras]
    in_specs = [lat(wq)] + ([ctx_in] if has_ctx else []) + [full(wq), full(wv)] + small
    args = [q] + ([q] if has_ctx else []) + [k, v] + list(extras)
    out_specs = [lat(wo)] + ([ctx_out] if has_ctx else [])
    out_shape = [jax.ShapeDtypeStruct((b, n_lat, heads * wo), BF16)] + (
        [jax.ShapeDtypeStruct((b, n_ctx, heads * wo), BF16)] if has_ctx else [])
    outs = pl.pallas_call(
        body,
        grid=(b, heads, nql + (1 if has_ctx else 0)),
        in_specs=in_specs,
        out_specs=out_specs,
        out_shape=out_shape,
        scratch_shapes=scratch,
        compiler_params=pltpu.CompilerParams(
            dimension_semantics=("parallel", "parallel", "arbitrary"), vmem_limit_bytes=VMEM_LIMIT),
        name=name,
    )(*args)
    return outs if has_ctx else (outs[0], None)


POOL_HALO = 16


def _pool_kernel(u_ref, w_ref, s_ref, o_ref, pad_ref, *, segments):
    for s0, n in segments:
        t = lax.broadcasted_iota(I32, (n, 1), 0)
        for g, w in enumerate(POOL_WINDOWS):
            cols = slice(g * POOL_GROUP_W, (g + 1) * POOL_GROUP_W)
            xg = u_ref[0, s0:s0 + n, cols].astype(F32)
            zeros = jnp.zeros((POOL_HALO, POOL_GROUP_W), F32)
            pad_ref[0:POOL_HALO, :] = zeros
            pad_ref[POOL_HALO + n:2 * POOL_HALO + n, :] = zeros
            pad_ref[POOL_HALO:POOL_HALO + n, :] = xg
            acc = pad_ref[POOL_HALO - w // 2:POOL_HALO - w // 2 + n, :]
            for j in range(1 - w // 2, w // 2):
                acc = acc + pad_ref[POOL_HALO + j:POOL_HALO + j + n, :]
            cnt = (jnp.minimum(t + w // 2, n) - jnp.maximum(t - w // 2, 0)).astype(F32)
            pooled = (acc / cnt - xg).astype(BF16)
            o_ref[0, s0:s0 + n, cols] = (_dot(pooled, w_ref[g]) * s_ref[:, cols]).astype(o_ref.dtype)


def _pool_mix(pool_in, w_pool, pool_scale, segments):
    b, lc, pw = pool_in.shape
    nmax = max(n for _, n in segments)
    lo = sum(n for _, n in segments)
    return pl.pallas_call(
        functools.partial(_pool_kernel, segments=segments),
        grid=(b,),
        in_specs=[
            pl.BlockSpec((1, lc, pw), lambda bi: (bi, 0, 0)),
            pl.BlockSpec(w_pool.shape, lambda bi: (0, 0, 0)),
            pl.BlockSpec(pool_scale.shape, lambda bi: (0, 0)),
        ],
        out_specs=pl.BlockSpec((1, lo, pw), lambda bi: (bi, 0, 0)),
        out_shape=jax.ShapeDtypeStruct((b, lo, pw), BF16),
        scratch_shapes=[pltpu.VMEM((nmax + 2 * POOL_HALO, POOL_GROUP_W), F32)],
        compiler_params=pltpu.CompilerParams(dimension_semantics=("parallel",), vmem_limit_bytes=VMEM_LIMIT),
        name="pool_mix",
    )(pool_in, w_pool, pool_scale)


def _merge_kernel(om_ref, od_ref, op_ref, gate_ref, h_ref, mod_ref, wb_ref, wo_ref, gpost_ref, gffn_ref, wr_ref,
                  h2_ref, fx_ref, lg_ref, *, n_lat_tiles, groups):
    d = h_ref.shape[-1]
    whi, wlo = _split_bf16(wr_ref[...])
    for g in range(groups):
        rows = slice(g * TM, (g + 1) * TM)
        is_lat = pl.program_id(1) * groups + g < n_lat_tiles
        m = jnp.where(is_lat, mod_ref[0, 1], mod_ref[0, 0])
        merged = (gate_ref[0, rows, 0:d].astype(F32) * _dot(om_ref[0, rows, :], wb_ref[0])
                  + gate_ref[0, rows, d:2 * d].astype(F32) * _dot(op_ref[0, rows, :], wb_ref[1])
                  + gate_ref[0, rows, 2 * d:3 * d].astype(F32) * _dot(od_ref[0, rows, :], wb_ref[2]))
        y = _dot(merged.astype(BF16), wo_ref[...])
        h2 = h_ref[0, rows, :] + m[2:3] * _rms(y, gpost_ref[...])
        h2_ref[0, rows, :] = h2
        fx = _rms(h2, gffn_ref[...]) * (1.0 + m[4:5]) + m[3:4]
        fhi, flo = _split_bf16(fx)
        fx_ref[0, rows, :] = fhi
        lg_ref[:, rows] = _dot_nt(whi, fhi) + _dot_nt(whi, flo) + _dot_nt(wlo, fhi)


def _merge(o_mla, o_diff, o_pool, gates, h, mods, wb, w_out, g_post, g_ffn, w_rt, n_lat, nt, groups):
    b, _, d = h.shape
    ne = w_rt.shape[0]
    assert nt % groups == 0
    steps = nt // groups
    tg = groups * TM
    tile = lambda w: pl.BlockSpec((1, tg, w), lambda bi, i: (bi, i, 0))
    return pl.pallas_call(
        functools.partial(_merge_kernel, n_lat_tiles=n_lat // TM, groups=groups),
        grid=(b, steps),
        in_specs=[
            tile(o_mla.shape[-1]), tile(o_diff.shape[-1]), tile(o_pool.shape[-1]), tile(gates.shape[-1]), tile(d),
            pl.BlockSpec((1, 2, 6, d), lambda bi, i: (bi, 0, 0, 0)),
            _const_spec(wb.shape), _const_spec(w_out.shape), _const_spec(g_post.shape), _const_spec(g_ffn.shape),
            _const_spec(w_rt.shape),
        ],
        out_specs=[tile(d), tile(d), pl.BlockSpec((ne, tg), lambda bi, i: (0, bi * steps + i))],
        out_shape=[jax.ShapeDtypeStruct((b, nt * TM, d), F32), jax.ShapeDtypeStruct((b, nt * TM, d), BF16),
                   jax.ShapeDtypeStruct((ne, b * nt * TM), F32)],
        compiler_params=pltpu.CompilerParams(
            dimension_semantics=("parallel", "parallel"), vmem_limit_bytes=VMEM_LIMIT),
        name="merge",
    )(o_mla, o_diff, o_pool, gates, h, mods, wb, w_out, g_post, g_ffn, w_rt)


def _route_kernel(lg_ref, br_ref, posm_ref, wts_ref, meta_ref, tot_ref, carry_ref):
    ne, tm = lg_ref.shape
    gsz = ne // N_GROUPS
    neg = -jnp.inf

    @pl.when(pl.program_id(0) == 0)
    def _():
        carry_ref[...] = jnp.zeros_like(carry_ref)

    scores = jax.nn.sigmoid(lg_ref[...])
    sel = scores + br_ref[...]

    jidx = lax.broadcasted_iota(I32, (gsz, tm), 0).astype(F32)
    gscore = []
    for g in range(N_GROUPS):
        v = sel[g * gsz:(g + 1) * gsz]
        m1 = jnp.max(v, axis=0, keepdims=True)
        first = jnp.min(jnp.where(v == m1, jidx, float(gsz)), axis=0, keepdims=True)
        m2 = jnp.max(jnp.where(jidx == first, neg, v), axis=0, keepdims=True)
        gscore.append(m1 + m2)
    masked = []
    for g in range(N_GROUPS):
        rank = jnp.zeros((1, tm), I32)
        for o in range(N_GROUPS):
            if o == g:
                continue
            ahead = gscore[o] > gscore[g]
            if o < g:
                ahead = ahead | (gscore[o] == gscore[g])
            rank = rank + ahead.astype(I32)
        masked.append(jnp.where(rank < TOPK_GROUPS, sel[g * gsz:(g + 1) * gsz], neg))
    cur = jnp.concatenate(masked, axis=0)

    eidx = lax.broadcasted_iota(I32, (ne, tm), 0).astype(F32)
    chosen = jnp.zeros((ne, tm), jnp.bool_)
    for _ in range(TOP_K):
        mx = jnp.max(cur, axis=0, keepdims=True)
        first = jnp.min(jnp.where(cur == mx, eidx, float(ne)), axis=0, keepdims=True)
        hit = eidx == first
        chosen = chosen | hit
        cur = jnp.where(hit, neg, cur)

    wsel = jnp.where(chosen, scores, 0.0)
    wts_ref[...] = wsel / jnp.sum(wsel, axis=0, keepdims=True) * ROUTED_SCALE

    cmask = chosen.astype(BF16)
    ti = lax.broadcasted_iota(I32, (tm, tm), 0)
    tj = lax.broadcasted_iota(I32, (tm, tm), 1)
    pos = _dot(cmask, (ti < tj).astype(BF16))
    posm_ref[...] = jnp.where(chosen, pos, -1.0)
    ei = lax.broadcasted_iota(I32, (ne, ne), 0)
    ej = lax.broadcasted_iota(I32, (ne, ne), 1)
    cnt = jnp.sum(chosen.astype(F32), axis=1, keepdims=True)
    cpad = jnp.floor((cnt + (ALIGN - 1)) * (1.0 / ALIGN)) * ALIGN
    cpad_b = jnp.broadcast_to(cpad, (ne, LANES))
    loc_b = _dot((ej < ei).astype(BF16), cpad_b.astype(BF16))

    gpos = carry_ref[...]
    lane = lax.broadcasted_iota(I32, (ne, LANES), 1)
    meta_ref[0] = jnp.where(lane == 0, cpad_b, jnp.where(lane == 1, loc_b, gpos)).astype(I32)
    carry_ref[...] = gpos + cpad_b
    tot_ref[...] = (gpos + cpad_b).astype(I32)


def _route(logits_t, b_router):
    ne, ntok = logits_t.shape
    ntiles = ntok // TM
    return pl.pallas_call(
        _route_kernel,
        grid=(ntiles,),
        in_specs=[pl.BlockSpec((ne, TM), lambda t: (0, t)), pl.BlockSpec((ne, 1), lambda t: (0, 0))],
        out_specs=[
            pl.BlockSpec((ne, TM), lambda t: (0, t)),
            pl.BlockSpec((ne, TM), lambda t: (0, t)),
            pl.BlockSpec((1, ne, LANES), lambda t: (t, 0, 0)),
            pl.BlockSpec((ne, LANES), lambda t: (0, 0)),
        ],
        out_shape=[
            jax.ShapeDtypeStruct((ne, ntok), F32),
            jax.ShapeDtypeStruct((ne, ntok), F32),
            jax.ShapeDtypeStruct((ntiles, ne, LANES), I32),
            jax.ShapeDtypeStruct((ne, LANES), I32),
        ],
        scratch_shapes=[pltpu.VMEM((ne, LANES), F32)],
        compiler_params=pltpu.CompilerParams(dimension_semantics=("arbitrary",)),
        name="route",
    )(logits_t, b_router)


WAIT_ROWS = 16 * ALIGN
START_UNROLL = 4


def _copy(local, remote, l0, r0, rows, sem, to_remote):
    lrows = local.at[pl.ds(l0, rows)]
    rrows = remote.at[pl.ds(r0, rows)]
    return pltpu.make_async_copy(lrows, rrows, sem) if to_remote else pltpu.make_async_copy(rrows, lrows, sem)


def _start_chunks(rem_ref, n, local, remote, sem, to_remote):
    def start(c):
        _copy(local, remote, pl.multiple_of(c * ALIGN, ALIGN), pl.multiple_of(rem_ref[0, 0, c], ALIGN), ALIGN,
              sem, to_remote).start()

    def per_group(g, carry):
        for u in range(START_UNROLL):
            start(g * START_UNROLL + u)
        return carry

    groups = n // START_UNROLL
    lax.fori_loop(0, groups, per_group, 0)
    lax.fori_loop(groups * START_UNROLL, n, lambda c, carry: (start(c), carry)[1], 0)


def _wait_chunks(n, local, remote, sem, to_remote):
    per = WAIT_ROWS // ALIGN
    lax.fori_loop(0, n // per, lambda i, c: (_copy(local, remote, 0, 0, WAIT_ROWS, sem, to_remote).wait(), c)[1], 0)
    lax.fori_loop(0, lax.rem(n, per), lambda i, c: (_copy(local, remote, 0, 0, ALIGN, sem, to_remote).wait(), c)[1], 0)


def _dispatch_kernel(n16_ref, tail_ref, tailn_ref, nused_ref, fx_ref, posm_ref, lohi_ref, rem_ref, xs_ref,
                     buf_ref, zero_ref, sem):
    t = pl.program_id(0)
    slot = lax.rem(t, 2)
    x = fx_ref[0]
    buf = buf_ref.at[slot]
    posm = posm_ref[...]
    m2 = jnp.concatenate([posm, jnp.zeros((LANES - posm.shape[0], posm.shape[1]), F32)], axis=0).astype(BF16)
    lo = lohi_ref[0, 0:1, :]
    hi = lohi_ref[0, 1:2, :]
    for rc in range(RL // RCH):
        r = (lax.broadcasted_iota(I32, (RCH, 1), 0) + rc * RCH).astype(F32)
        member = (r >= lo) & (r < hi)
        base = jnp.sum(jnp.where(member, lo, 0.0), axis=1, keepdims=True)
        onehot = (_dot(member.astype(BF16), m2) == r - base).astype(BF16)
        buf[rc * RCH:(rc + 1) * RCH, :] = _dot(onehot, x).astype(buf_ref.dtype)

    @pl.when(t > 0)
    def _():
        _wait_chunks(n16_ref[t - 1], buf_ref.at[1 - slot], xs_ref, sem.at[0], True)

    _start_chunks(rem_ref, n16_ref[t], buf, xs_ref, sem.at[0], True)

    def tail(act):
        def go(cp):
            if act == 0:
                cp.start()
            else:
                cp.wait()

        def per_expert(e, carry):
            s0 = tail_ref[e]

            def per_chunk(j, c):
                go(pltpu.make_async_copy(
                    zero_ref.at[pl.ds(0, ALIGN)],
                    xs_ref.at[pl.ds(pl.multiple_of(s0 + j * ALIGN, ALIGN), ALIGN)], sem.at[1]))
                return c

            lax.fori_loop(0, tailn_ref[e], per_chunk, 0)
            return carry

        lax.fori_loop(0, N_EXPERTS, per_expert, 0)

        def per_block(j, c):
            go(pltpu.make_async_copy(zero_ref, xs_ref.at[pl.ds(pl.multiple_of(j * EBLK, EBLK), EBLK)], sem.at[1]))
            return c

        lax.fori_loop(nused_ref[0], xs_ref.shape[0] // EBLK, per_block, 0)

    @pl.when(t == pl.num_programs(0) - 1)
    def _():
        zero_ref[...] = jnp.zeros_like(zero_ref)
        tail(0)
        _wait_chunks(n16_ref[t], buf, xs_ref, sem.at[0], True)
        tail(1)


def _dispatch(fx, posm, lohi, tabs, rmax):
    b, rows, d = fx.shape
    nt = rows // TM
    ntiles = b * nt
    rem = tabs["rem"]
    return pl.pallas_call(
        _dispatch_kernel,
        grid_spec=pltpu.PrefetchScalarGridSpec(
            num_scalar_prefetch=4,
            grid=(ntiles,),
            in_specs=[
                pl.BlockSpec((1, TM, d), lambda t, *_: (t // nt, t % nt, 0)),
                pl.BlockSpec((posm.shape[0], TM), lambda t, *_: (0, t)),
                pl.BlockSpec((1,) + lohi.shape[1:], lambda t, *_: (t, 0, 0)),
                pl.BlockSpec((1,) + rem.shape[1:], lambda t, *_: (t, 0, 0), memory_space=pltpu.SMEM),
            ],
            out_specs=pl.BlockSpec(memory_space=pl.ANY),
            scratch_shapes=[pltpu.VMEM((2, RL, d), BF16), pltpu.VMEM((EBLK, d), BF16),
                            pltpu.SemaphoreType.DMA((2,))],
        ),
        out_shape=jax.ShapeDtypeStruct((rmax, d), BF16),
        compiler_params=pltpu.CompilerParams(dimension_semantics=("arbitrary",), vmem_limit_bytes=VMEM_LIMIT),
        name="dispatch",
    )(tabs["n16"], tabs["tail_start"], tabs["tail_n16"], tabs["n_used"], fx, posm, lohi, rem)


def _expert_kernel(blk_ref, nused_ref, x_ref, wg_ref, wu_ref, wd_ref, y_ref, wgu_bf, wd_bf):
    i = pl.program_id(0)

    @pl.when((i == 0) | (blk_ref[i] != blk_ref[jnp.maximum(i - 1, 0)]))
    def _():
        wgu_bf[:, :EXPERT_FF] = wg_ref[0].astype(BF16)
        wgu_bf[:, EXPERT_FF:] = wu_ref[0].astype(BF16)
        wd_bf[...] = wd_ref[0].astype(BF16)

    @pl.when(i < nused_ref[0])
    def _():
        gu = _dot(x_ref[...], wgu_bf[...])
        hmid = (_silu(gu[:, :EXPERT_FF]) * gu[:, EXPERT_FF:]).astype(BF16)
        y_ref[...] = _dot(hmid, wd_bf[...]).astype(y_ref.dtype)


def _experts(xs, blk_e, n_used, w_g, w_u, w_d, layer):
    rmax, d = xs.shape
    nblk = rmax // EBLK
    row = lambda i, blk, nu: (jnp.minimum(i, nu[0] - 1), 0)
    wsel = lambda i, blk, nu: (layer, blk[i], 0, 0)
    return pl.pallas_call(
        _expert_kernel,
        grid_spec=pltpu.PrefetchScalarGridSpec(
            num_scalar_prefetch=2,
            grid=(nblk,),
            in_specs=[
                pl.BlockSpec((EBLK, d), row),
                pl.BlockSpec((None, 1, d, EXPERT_FF), wsel),
                pl.BlockSpec((None, 1, d, EXPERT_FF), wsel),
                pl.BlockSpec((None, 1, EXPERT_FF, d), wsel),
            ],
            out_specs=pl.BlockSpec((EBLK, d), row),
            scratch_shapes=[pltpu.VMEM((d, 2 * EXPERT_FF), BF16), pltpu.VMEM((EXPERT_FF, d), BF16)],
        ),
        out_shape=jax.ShapeDtypeStruct((rmax, d), BF16),
        input_output_aliases={2: 0},
        compiler_params=pltpu.CompilerParams(dimension_semantics=("arbitrary",), vmem_limit_bytes=VMEM_LIMIT),
        name="experts",
    )(blk_e, n_used, xs, w_g, w_u, w_d)


def _combine_kernel(n16_ref, y_ref, posmt_ref, wt_ref, meta_ref, fx_ref, h_ref, mod_ref, wsgu_ref,
                    wsd_ref, gpost_ref, rem_ref, remn_ref, o_ref, buf_ref, sem):
    t = pl.program_id(0)
    slot = lax.rem(t, 2)
    buf = buf_ref.at[slot]

    @pl.when(t == 0)
    def _():
        buf_ref[...] = jnp.zeros_like(buf_ref)
        _start_chunks(rem_ref, n16_ref[t], buf, y_ref, sem.at[slot], False)

    @pl.when(t + 1 < pl.num_programs(0))
    def _():
        _start_chunks(remn_ref, n16_ref[t + 1], buf_ref.at[1 - slot], y_ref, sem.at[1 - slot], False)

    x = fx_ref[0]
    gu = _dot(x, wsgu_ref[...])
    ff = wsd_ref.shape[0]
    o_ref[0] = _dot((_silu(gu[:, :ff]) * gu[:, ff:]).astype(BF16), wsd_ref[...])

    _wait_chunks(n16_ref[t], buf, y_ref, sem.at[slot], False)

    posmt = posmt_ref[...].astype(BF16)
    wt = wt_ref[...].astype(BF16)
    lo = meta_ref[0, :, 1:2].astype(F32)
    hi = lo + meta_ref[0, :, 0:1].astype(F32)
    for rc in range(RL // RCH):
        r = (lax.broadcasted_iota(I32, (1, RCH), 1) + rc * RCH).astype(F32)
        member = (r >= lo) & (r < hi)
        base = jnp.sum(jnp.where(member, lo, 0.0), axis=0, keepdims=True)
        mb = member.astype(BF16)
        pw = jnp.where(_dot(posmt, mb) == r - base, _dot(wt, mb), 0.0).astype(BF16)
        o_ref[0] += _dot(pw, buf[rc * RCH:(rc + 1) * RCH, :])

    m = mod_ref[0, 0]
    o_ref[0] = h_ref[0] + m[5:6] * _rms(o_ref[0], gpost_ref[...])


def _combine(y, posm_t, wts_t, meta, fx, h2, mods, tabs, w_sgu, w_sd, g_post, n_lat):
    b, rows, d = fx.shape
    nt = rows // TM
    ntiles = b * nt
    nlt = n_lat // TM
    ne = posm_t.shape[1]
    rem = tabs["rem"]
    tile = lambda t, *_: (t // nt, t % nt, 0)
    return pl.pallas_call(
        _combine_kernel,
        grid_spec=pltpu.PrefetchScalarGridSpec(
            num_scalar_prefetch=1,
            grid=(ntiles,),
            in_specs=[
                pl.BlockSpec(memory_space=pl.ANY),
                pl.BlockSpec((TM, ne), lambda t, *_: (t, 0)),
                pl.BlockSpec((TM, ne), lambda t, *_: (t, 0)),
                pl.BlockSpec((1,) + meta.shape[1:], lambda t, *_: (t, 0, 0)),
                pl.BlockSpec((1, TM, d), tile),
                pl.BlockSpec((1, TM, d), tile),
                pl.BlockSpec((1, 1, 6, d), lambda t, *_: (t // nt, jnp.where(t % nt < nlt, 1, 0), 0, 0)),
                pl.BlockSpec(w_sgu.shape, lambda t, *_: (0, 0), pipeline_mode=pl.Buffered(1)),
                pl.BlockSpec(w_sd.shape, lambda t, *_: (0, 0), pipeline_mode=pl.Buffered(1)),
                pl.BlockSpec(g_post.shape, lambda t, *_: (0, 0), pipeline_mode=pl.Buffered(1)),
                pl.BlockSpec((1,) + rem.shape[1:], lambda t, *_: (t, 0, 0), memory_space=pltpu.SMEM),
                pl.BlockSpec((1,) + rem.shape[1:], lambda t, *_: (jnp.minimum(t + 1, ntiles - 1), 0, 0),
                             memory_space=pltpu.SMEM),
            ],
            out_specs=pl.BlockSpec((1, TM, d), tile),
            scratch_shapes=[pltpu.VMEM((2, RL, d), BF16), pltpu.SemaphoreType.DMA((2,))],
        ),
        out_shape=jax.ShapeDtypeStruct((b, rows, d), F32),
        compiler_params=pltpu.CompilerParams(dimension_semantics=("arbitrary",), vmem_limit_bytes=VMEM_LIMIT),
        name="combine",
    )(tabs["n16"], y, posm_t, wts_t, meta, fx, h2, mods, w_sgu, w_sd, g_post, rem, rem)


def _rope_tables(n_lat, n_ctx, rot_dim, lane_slots):
    rows = n_lat // GRID_W
    pos_row = jnp.repeat(jnp.arange(rows, dtype=F32), GRID_W)
    pos_col = jnp.tile(jnp.arange(GRID_W, dtype=F32), rows)
    d_axis = rot_dim // 2
    inv_freq = ROPE_BASE ** (-jnp.arange(0, d_axis, 2, dtype=F32) / d_axis)
    ar = pos_row[:, None] * inv_freq
    ac = pos_col[:, None] * inv_freq
    ang = jnp.concatenate([ar, ar, ac, ac], axis=1)
    qtr = rot_dim // 4
    first = (jnp.arange(rot_dim) // qtr) % 2 == 0
    cos, sin = jnp.cos(ang), jnp.sin(ang)
    s1 = jnp.where(first, -sin, 0.0)
    s2 = jnp.where(first, 0.0, sin)
    lc = n_lat + n_ctx
    c_t = jnp.ones((lc, LANES), F32)
    s1_t = jnp.zeros((lc, LANES), F32)
    s2_t = jnp.zeros((lc, LANES), F32)
    for lo in lane_slots:
        c_t = c_t.at[:n_lat, lo:lo + rot_dim].set(cos)
        s1_t = s1_t.at[:n_lat, lo:lo + rot_dim].set(s1)
        s2_t = s2_t.at[:n_lat, lo:lo + rot_dim].set(s2)
    return jnp.stack([c_t, s1_t, s2_t])


def _pack_layer_weights(w_in, w_uq, w_ukv):
    d = w_in.shape[0]
    kv0 = 0
    kr0 = kv0 + MLA_KV_RANK
    kd0 = kr0 + MLA_ROPE
    vd0 = kd0 + 512
    q0 = vd0 + 512
    qd0 = q0 + MLA_Q_RANK
    p0 = qd0 + 512
    g0 = p0 + 512
    kr = jnp.zeros((d, LANES), w_in.dtype).at[:, MLA_NOPE:MLA_NOPE + MLA_ROPE].set(w_in[:, kr0:kd0])
    w1 = jnp.concatenate([w_in[:, kv0:kr0], kr, w_in[:, kd0:vd0], w_in[:, vd0:q0], w_in[:, q0:qd0],
                          w_in[:, qd0:p0], w_in[:, p0:g0], w_in[:, g0:]], axis=1).astype(BF16)
    ukv = w_ukv.reshape(MLA_KV_RANK, MLA_HEADS, MLA_NOPE + MLA_V)
    uk = jnp.pad(ukv[:, :, :MLA_NOPE], ((0, 0), (0, 0), (0, LANES - MLA_NOPE))).reshape(MLA_KV_RANK, -1)
    uv = ukv[:, :, MLA_NOPE:].reshape(MLA_KV_RANK, -1)
    w_ukv_p = jnp.concatenate([uk, uv], axis=1).astype(BF16)
    uq = w_uq.reshape(MLA_Q_RANK, MLA_HEADS, MLA_NOPE + MLA_ROPE)
    w_uq_p = jnp.pad(uq, ((0, 0), (0, 0), (0, LANES - MLA_NOPE - MLA_ROPE))).reshape(MLA_Q_RANK, -1).astype(BF16)
    return w1, w_ukv_p, w_uq_p


def _routing_tables(meta, totals, rmax):
    cpad, loc, gpos = meta[:, :, 0], meta[:, :, 1], meta[:, :, 2]
    tot = totals[:, 0]
    reg = (tot + EBLK - 1) // EBLK * EBLK
    reg_end = jnp.cumsum(reg)
    reg_start = reg_end - reg
    n_used = jnp.maximum(reg_end[-1:] // EBLK, 1).astype(I32)
    blk_start = jnp.arange(rmax // EBLK, dtype=I32) * EBLK
    blk_e = jnp.minimum(jnp.sum(reg_end[None, :] <= blk_start[:, None], axis=1), N_EXPERTS - 1).astype(I32)
    ntiles, ne = loc.shape
    lohi = jnp.zeros((ntiles, 8, LANES), F32).at[:, 0, :].set(float(RL)).at[:, 0, :ne].set(loc.astype(F32))
    lohi = lohi.at[:, 1, :ne].set((loc + cpad).astype(F32))
    chunk = jnp.arange(RL // ALIGN, dtype=I32)[None, :, None]
    seg_lo = (loc // ALIGN)[:, None, :]
    seg_end = (loc + cpad) // ALIGN
    inside = (seg_lo <= chunk) & (chunk < seg_end[:, None, :])
    shift = jnp.sum(jnp.where(inside, (reg_start[None, :] + gpos - loc)[:, None, :], 0), axis=-1)
    rem = (shift + chunk[:, :, 0] * ALIGN).astype(I32)[:, None, :]
    return dict(
        n16=seg_end[:, -1].astype(I32), rem=rem,
        tail_start=(reg_start + tot).astype(I32), tail_n16=((reg - tot) // ALIGN).astype(I32),
        n_used=n_used), blk_e, lohi


def kernel(x, c, ctx, c_ctx, w_mod, b_mod, g_pre_mix, g_post_mix, g_pre_ffn, g_post_ffn, w_in, g_cq, w_uq, g_ckv, w_ukv, w_pool, pool_scale, lam_q1, lam_k1, lam_q2, lam_k2, g_subln, w_b_mla, w_b_pool, w_b_diff, w_out, w_router, b_router, w_e_gate, w_e_up, w_e_down, w_s_gate, w_s_up, w_s_down):
    b, n_lat, d = x.shape
    n_ctx = ctx.shape[1]
    depth = w_mod.shape[0]
    assert n_ctx == TM and n_lat % TQ == 0 and n_lat % GRID_W == 0
    lc = n_lat + n_ctx
    row2 = lambda v: v.reshape(1, -1)

    rows = (b + 1 + 7) // 8 * 8
    cvec = jnp.zeros((rows, d), F32).at[:b].set(c).at[b].set(c_ctx)
    mod_all = _modulation(cvec, w_mod, b_mod).reshape(depth, rows, 6, d)

    tab_m = _rope_tables(n_lat, n_ctx, MLA_ROPE, (MLA_NOPE,))
    tab_d = _rope_tables(n_lat, n_ctx, DIFF_HALF, (0, DIFF_HALF))

    h = jnp.concatenate([x, ctx], axis=1)
    for l in range(depth):
        last = l == depth - 1
        lam_init = 0.8 - 0.6 * math.exp(-0.3 * l)
        mods = jnp.stack([jnp.broadcast_to(mod_all[l, b], (b, 6, d)), mod_all[l, :b]], axis=1)
        w1, w_ukv_p, w_uq_p = _pack_layer_weights(w_in[l], w_uq[l], w_ukv[l])

        qm, km, vm, qd, kd, vd, pool_in, gates = _project(
            h, mods, row2(g_pre_mix[l]), w1, row2(g_ckv[l]), w_ukv_p, row2(g_cq[l]), w_uq_p, tab_m, tab_d, n_lat)
        o_mla, o_mla_ctx = _attention(
            functools.partial(_mla_kernel, n_lat=n_lat, has_ctx=not last), "mla_attention", qm, km, vm, (),
            [pltpu.VMEM((2, lc, 2 * MLA_V), BF16)], n_lat, not last, MLA_HEADS // 2, 2 * LANES, 2 * MLA_V, 2 * MLA_V)
        lam_p = jnp.stack([lam_q1[l], lam_k1[l], lam_q2[l], lam_k2[l]])
        o_diff, o_diff_ctx = _attention(
            functools.partial(_diff_kernel, n_lat=n_lat, has_ctx=not last, lam_init=lam_init), "diff_attention",
            qd, kd, vd, (lam_p, row2(g_subln[l])), [], n_lat, not last, DIFF_HEADS, LANES, DIFF_V, DIFF_V)
        segments = ((0, n_lat),) if last else ((0, n_lat), (n_lat, n_ctx))
        o_pool = _pool_mix(pool_in, w_pool[l].astype(BF16), row2(pool_scale[l]), segments)

        nt = (n_lat if last else lc) // TM
        if not last:
            o_mla = jnp.concatenate([o_mla, o_mla_ctx], axis=1)
            o_diff = jnp.concatenate([o_diff, o_diff_ctx], axis=1)
        groups = next(g for g in (4, 3, 2, 1) if nt % g == 0)
        wb = jnp.stack([w_b_mla[l], w_b_pool[l], w_b_diff[l]]).astype(BF16)
        h2, fx, logits_t = _merge(o_mla, o_diff, o_pool, gates, h, mods, wb, w_out[l].astype(BF16),
                                  row2(g_post_mix[l]), row2(g_pre_ffn[l]), w_router[l].T, n_lat, nt, groups)

        posm, wts, meta, totals = _route(logits_t, b_router[l].reshape(-1, 1))
        rmax = meta.shape[0] * RL + N_EXPERTS * EBLK
        tabs, blk_e, lohi = _routing_tables(meta, totals, rmax)
        xs = _dispatch(fx, posm, lohi, tabs, rmax)
        ys = _experts(xs, blk_e, tabs["n_used"], w_e_gate, w_e_up, w_e_down, l)
        w_sgu = jnp.concatenate([w_s_gate[l], w_s_up[l]], axis=-1).astype(BF16)
        h = _combine(ys, posm.T, wts.T, meta, fx, h2, mods, tabs, w_sgu, w_s_down[l].astype(BF16),
                     row2(g_post_ffn[l]), n_lat)
    return h
```

```python
import functools
import math

import jax
import jax.numpy as jnp
from jax import lax
from jax.experimental import pallas as pl
from jax.experimental.pallas import tpu as pltpu

F32 = jnp.float32
BF16 = jnp.bfloat16
I32 = jnp.int32

EPS = 1e-6
ROPE_BASE = 10000.0
GRID_W = 64

MLA_HEADS = 8
MLA_NOPE = 64
MLA_ROPE = 32
MLA_V = 64
MLA_Q_RANK = 384
MLA_KV_RANK = 256
POOL_WINDOWS = (2, 4, 8, 16)
POOL_GROUP_W = 128
DIFF_HEADS = 4
DIFF_HALF = 64
DIFF_V = 128
N_EXPERTS = 64
TOP_K = 8
N_GROUPS = 8
TOPK_GROUPS = 4
EXPERT_FF = 256
ROUTED_SCALE = 2.5

LANES = 128
TM = 256
TQ = 2048
QSPLIT = 8
ALIGN = 16
EBLK = 1024
RL = TM * TOP_K + N_EXPERTS * ALIGN
RCH = 512
VMEM_LIMIT = 56 * 1024 * 1024
LOG2E = 1.4426950408889634

C_CKV = 0
C_KR = C_CKV + MLA_KV_RANK
C_KD = C_KR + LANES
C_VD = C_KD + 512
C_CQ = C_VD + 512
C_QD = C_CQ + MLA_Q_RANK
C_POOL = C_QD + 512
C_GATE = C_POOL + 512


def _rms(x, g):
    ms = jnp.mean(x * x, axis=-1, keepdims=True)
    return x * lax.rsqrt(ms + EPS) * g


def _dot(a, b):
    return jnp.dot(a, b, preferred_element_type=F32)


def _dot_nt(a, b):
    return lax.dot_general(a, b, (((1,), (1,)), ((), ())), preferred_element_type=F32)


def _split_bf16(x):
    hi = x.astype(BF16)
    lo = (x - hi.astype(F32)).astype(BF16)
    return hi, lo


def _silu(x):
    return x * jax.nn.sigmoid(x)


def _rope(x, tab, quarter):
    w = x.shape[-1]
    rep = w // LANES
    c, s1, s2 = (jnp.tile(t, (1, rep)) if rep > 1 else t for t in tab)
    return x * c + pltpu.roll(x, w - quarter, 1) * s1 + pltpu.roll(x, quarter, 1) * s2


def _const_spec(shape):
    nd = len(shape)
    return pl.BlockSpec(shape, lambda *_: (0,) * nd, pipeline_mode=pl.Buffered(1))


def _mod_kernel(a_ref, w_ref, b_ref, o_ref):
    a = _silu(a_ref[...])
    ahi, alo = _split_bf16(a)
    whi, wlo = _split_bf16(w_ref[...])
    o_ref[...] = _dot(ahi, whi) + _dot(alo, whi) + _dot(ahi, wlo) + b_ref[...]


def _modulation(cvec, w_mod, b_mod):
    nl, d, d6 = w_mod.shape
    rows = cvec.shape[0]
    nc = d6 // d
    return pl.pallas_call(
        _mod_kernel,
        grid=(nl, nc),
        in_specs=[
            pl.BlockSpec((rows, d), lambda l, j: (0, 0)),
            pl.BlockSpec((None, d, d), lambda l, j: (l, 0, j)),
            pl.BlockSpec((None, 1, d), lambda l, j: (l, 0, j)),
        ],
        out_specs=pl.BlockSpec((None, rows, d), lambda l, j: (l, 0, j)),
        out_shape=jax.ShapeDtypeStruct((nl, rows, d6), F32),
        name="modulation",
    )(cvec, w_mod, b_mod.reshape(nl, 1, d6))


PROJ_GROUP = 2


def _proj_kernel(h_ref, mod_ref, gpre_ref, w1_ref, gckv_ref, wukv_ref, gcq_ref, wuq_ref, tm_ref, td_ref,
                 qm_ref, km_ref, vm_ref, qd_ref, kd_ref, vd_ref, pool_ref, gate_ref, *, tiles_per_sample,
                 n_lat_tiles, groups):
    d = h_ref.shape[-1]
    hx, tabm, tabd = [], [], []
    for g in range(groups):
        tile = pl.program_id(0) * groups + g
        sample = tile // tiles_per_sample
        pos_tile = tile - sample * tiles_per_sample
        m = jnp.where(pos_tile < n_lat_tiles, mod_ref[sample, 1], mod_ref[sample, 0])
        pos = pl.ds(pl.multiple_of(pos_tile * TM, TM), TM)
        tabm.append([tm_ref[k, pos, :] for k in range(3)])
        tabd.append([td_ref[k, pos, :] for k in range(3)])
        hx.append((_rms(h_ref[g * TM:(g + 1) * TM, :], gpre_ref[...]) * (1.0 + m[1:2]) + m[0:1]).astype(BF16))
    hx = jnp.concatenate(hx, axis=0)
    tabm = tuple(jnp.concatenate([tg[k] for tg in tabm], axis=0) for k in range(3))
    tabd = tuple(jnp.concatenate([tg[k] for tg in tabd], axis=0) for k in range(3))

    ckv = _dot(hx, w1_ref[:, C_CKV:C_KR])
    kv = _dot(_rms(ckv, gckv_ref[...]).astype(BF16), wukv_ref[...])
    kr = _rope(_dot(hx, w1_ref[:, C_KR:C_KD]), tabm, MLA_ROPE // 4)
    nk = MLA_HEADS * LANES
    km_ref[...] = (kv[:, :nk] + jnp.tile(kr, (1, MLA_HEADS))).astype(BF16)
    vm_ref[...] = kv[:, nk:].astype(BF16)

    kd_ref[...] = _rope(_dot(hx, w1_ref[:, C_KD:C_VD]), tabd, DIFF_HALF // 4).astype(BF16)
    vd_ref[...] = _dot(hx, w1_ref[:, C_VD:C_CQ]).astype(BF16)

    cq = _dot(hx, w1_ref[:, C_CQ:C_QD])
    q = _dot(_rms(cq, gcq_ref[...]).astype(BF16), wuq_ref[...])
    qm_scale = LOG2E / math.sqrt(MLA_NOPE + MLA_ROPE)
    qm_ref[...] = (_rope(q, tabm, MLA_ROPE // 4) * qm_scale).astype(BF16)
    qd_scale = LOG2E / math.sqrt(DIFF_HALF)
    qd_ref[...] = (_rope(_dot(hx, w1_ref[:, C_QD:C_POOL]), tabd, DIFF_HALF // 4) * qd_scale).astype(BF16)
    pool_ref[...] = _dot(hx, w1_ref[:, C_POOL:C_GATE]).astype(BF16)
    for j in range(3):
        lo = C_GATE + j * d
        gate_ref[:, j * d:(j + 1) * d] = jax.nn.sigmoid(_dot(hx, w1_ref[:, lo:lo + d])).astype(BF16)


def _project(h, mods, g_pre, w1, g_ckv, w_ukv, g_cq, w_uq, tab_m, tab_d, n_lat):
    b, lc, d = h.shape
    nt = lc // TM
    groups = PROJ_GROUP if (b * nt) % PROJ_GROUP == 0 else 1
    tile = lambda w: pl.BlockSpec((groups * TM, w), lambda i: (i, 0))
    widths = (MLA_HEADS * LANES, MLA_HEADS * LANES, MLA_HEADS * MLA_V, 512, 512, 512, 512, 3 * d)
    outs = pl.pallas_call(
        functools.partial(_proj_kernel, tiles_per_sample=nt, n_lat_tiles=n_lat // TM, groups=groups),
        grid=(b * nt // groups,),
        in_specs=[tile(d)] + [_const_spec(a.shape) for a in (mods, g_pre, w1, g_ckv, w_ukv, g_cq, w_uq, tab_m, tab_d)],
        out_specs=[tile(w) for w in widths],
        out_shape=[jax.ShapeDtypeStruct((b * lc, w), BF16) for w in widths],
        compiler_params=pltpu.CompilerParams(dimension_semantics=("parallel",), vmem_limit_bytes=VMEM_LIMIT),
        name="project",
    )(h.reshape(b * lc, d), mods, g_pre, w1, g_ckv, w_ukv, g_cq, w_uq, tab_m, tab_d)
    return [o.reshape(b, lc, -1) for o in outs]


def _exp_scores(q, k):
    s = _dot_nt(q, k)
    return jnp.exp2(s - jnp.max(s, axis=-1, keepdims=True))


def _attend_steps(attend, n_lat_steps, n_lat, lc, refs_lat, refs_ctx):
    i = pl.program_id(2)
    if refs_ctx is None:
        attend(*refs_lat, 0, lc)
    else:
        pl.when(i < n_lat_steps)(lambda: attend(*refs_lat, 0, lc))
        pl.when(i == n_lat_steps)(lambda: attend(*refs_ctx, n_lat, lc))


def _mla_kernel(*refs, n_lat, has_ctx):
    if has_ctx:
        ql_ref, qc_ref, k_ref, v_ref, ol_ref, oc_ref, vext_ref = refs
    else:
        ql_ref, k_ref, v_ref, ol_ref, vext_ref = refs
    lc = k_ref.shape[1]
    lane = lax.broadcasted_iota(I32, (1, LANES), 1)

    @pl.when(pl.program_id(2) == 0)
    def _():
        v = v_ref[0]
        one = jnp.ones_like(v)
        vext_ref[0] = jnp.where(lane < MLA_V, v, one)
        vext_ref[1] = jnp.where(lane < MLA_V, one, v)

    def attend(q_ref, o_ref, k0, k1):
        rows = q_ref.shape[1]
        step = rows // QSPLIT if rows >= QSPLIT * LANES else rows
        for r0 in range(0, rows, step):
            pv = []
            for j in range(2):
                e = _exp_scores(q_ref[0, r0:r0 + step, j * LANES:(j + 1) * LANES],
                                k_ref[0, k0:k1, j * LANES:(j + 1) * LANES])
                pv.append(_dot(e.astype(BF16), vext_ref[j, k0:k1, :]))
            o0 = pv[0] * (1.0 / pv[0][:, MLA_V:MLA_V + 1])
            o1 = pv[1] * (1.0 / pv[1][:, 0:1])
            o_ref[0, r0:r0 + step, :] = jnp.where(lane < MLA_V, o0, o1).astype(o_ref.dtype)

    _attend_steps(attend, n_lat // TQ, n_lat, lc, (ql_ref, ol_ref), (qc_ref, oc_ref) if has_ctx else None)


def _diff_kernel(*refs, n_lat, has_ctx, lam_init):
    if has_ctx:
        ql_ref, qc_ref, k_ref, v_ref, lam_ref, g_ref, ol_ref, oc_ref = refs
    else:
        ql_ref, k_ref, v_ref, lam_ref, g_ref, ol_ref = refs
    lc = k_ref.shape[1]
    lane = lax.broadcasted_iota(I32, (1, LANES), 1)
    lp = lam_ref[...]
    lam = (jnp.exp(jnp.sum(lp[0:1] * lp[1:2], axis=-1, keepdims=True))
           - jnp.exp(jnp.sum(lp[2:3] * lp[3:4], axis=-1, keepdims=True)) + lam_init)

    def attend(q_ref, o_ref, k0, k1):
        rows = q_ref.shape[1]
        step = rows // QSPLIT if rows >= QSPLIT * LANES else rows
        k = k_ref[0, k0:k1, :]
        v = v_ref[0, k0:k1, :]
        for r0 in range(0, rows, step):
            q = q_ref[0, r0:r0 + step, :]
            zero = jnp.zeros_like(q)
            e0 = _exp_scores(jnp.where(lane < DIFF_HALF, q, zero), k)
            e1 = _exp_scores(jnp.where(lane < DIFF_HALF, zero, q), k)
            o = (_dot(e0.astype(BF16), v) * (1.0 / jnp.sum(e0, axis=-1, keepdims=True))
                 - _dot(e1.astype(BF16), v) * (lam / jnp.sum(e1, axis=-1, keepdims=True)))
            o_ref[0, r0:r0 + step, :] = (_rms(o, g_ref[...]) * (1.0 - lam_init)).astype(o_ref.dtype)

    _attend_steps(attend, n_lat // TQ, n_lat, lc, (ql_ref, ol_ref), (qc_ref, oc_ref) if has_ctx else None)


def _attention(body, name, q, k, v, extras, scratch, n_lat, has_ctx, heads, wq, wv, wo):
    b, lc, _ = q.shape
    n_ctx = lc - n_lat
    nql = n_lat // TQ
    lat = lambda w: pl.BlockSpec((1, TQ, w), lambda bi, h, i: (bi, jnp.minimum(i, nql - 1), h))
    ctx_in = pl.BlockSpec((1, n_ctx, wq), lambda bi, h, i: (bi, n_lat // n_ctx, h))
    ctx_out = pl.BlockSpec((1, n_ctx, wo), lambda bi, h, i: (bi, 0, h))
    full = lambda w: pl.BlockSpec((1, lc, w), lambda bi, h, i: (bi, 0, h))
    small = [pl.BlockSpec(a.shape, lambda bi, h, i: (0, 0)) for a in extras]
    in_specs = [lat(wq)] + ([ctx_in] if has_ctx else []) + [full(wq), full(wv)] + small
    args = [q] + ([q] if has_ctx else []) + [k, v] + list(extras)
    out_specs = [lat(wo)] + ([ctx_out] if has_ctx else [])
    out_shape = [jax.ShapeDtypeStruct((b, n_lat, heads * wo), BF16)] + (
        [jax.ShapeDtypeStruct((b, n_ctx, heads * wo), BF16)] if has_ctx else [])
    outs = pl.pallas_call(
        body,
        grid=(b, heads, nql + (1 if has_ctx else 0)),
        in_specs=in_specs,
        out_specs=out_specs,
        out_shape=out_shape,
        scratch_shapes=scratch,
        compiler_params=pltpu.CompilerParams(
            dimension_semantics=("parallel", "parallel", "arbitrary"), vmem_limit_bytes=VMEM_LIMIT),
        name=name,
    )(*args)
    return outs if has_ctx else (outs[0], None)


POOL_HALO = 16


def _pool_kernel(u_ref, w_ref, s_ref, o_ref, pad_ref, *, segments):
    for s0, n in segments:
        t = lax.broadcasted_iota(I32, (n, 1), 0)
        for g, w in enumerate(POOL_WINDOWS):
            cols = slice(g * POOL_GROUP_W, (g + 1) * POOL_GROUP_W)
            xg = u_ref[0, s0:s0 + n, cols].astype(F32)
            zeros = jnp.zeros((POOL_HALO, POOL_GROUP_W), F32)
            pad_ref[0:POOL_HALO, :] = zeros
            pad_ref[POOL_HALO + n:2 * POOL_HALO + n, :] = zeros
            pad_ref[POOL_HALO:POOL_HALO + n, :] = xg
            acc = pad_ref[POOL_HALO - w // 2:POOL_HALO - w // 2 + n, :]
            for j in range(1 - w // 2, w // 2):
                acc = acc + pad_ref[POOL_HALO + j:POOL_HALO + j + n, :]
            cnt = (jnp.minimum(t + w // 2, n) - jnp.maximum(t - w // 2, 0)).astype(F32)
            pooled = (acc / cnt - xg).astype(BF16)
            o_ref[0, s0:s0 + n, cols] = (_dot(pooled, w_ref[g]) * s_ref[:, cols]).astype(o_ref.dtype)


def _pool_mix(pool_in, w_pool, pool_scale, segments):
    b, lc, pw = pool_in.shape
    nmax = max(n for _, n in segments)
    lo = sum(n for _, n in segments)
    return pl.pallas_call(
        functools.partial(_pool_kernel, segments=segments),
        grid=(b,),
        in_specs=[
            pl.BlockSpec((1, lc, pw), lambda bi: (bi, 0, 0)),
            pl.BlockSpec(w_pool.shape, lambda bi: (0, 0, 0)),
            pl.BlockSpec(pool_scale.shape, lambda bi: (0, 0)),
        ],
        out_specs=pl.BlockSpec((1, lo, pw), lambda bi: (bi, 0, 0)),
        out_shape=jax.ShapeDtypeStruct((b, lo, pw), BF16),
        scratch_shapes=[pltpu.VMEM((nmax + 2 * POOL_HALO, POOL_GROUP_W), F32)],
        compiler_params=pltpu.CompilerParams(dimension_semantics=("parallel",), vmem_limit_bytes=VMEM_LIMIT),
        name="pool_mix",
    )(pool_in, w_pool, pool_scale)


def _merge_kernel(om_ref, od_ref, op_ref, gate_ref, h_ref, mod_ref, wb_ref, wo_ref, gpost_ref, gffn_ref, wr_ref,
                  h2_ref, fx_ref, lg_ref, *, n_lat_tiles, groups):
    d = h_ref.shape[-1]
    whi, wlo = _split_bf16(wr_ref[...])
    for g in range(groups):
        rows = slice(g * TM, (g + 1) * TM)
        is_lat = pl.program_id(1) * groups + g < n_lat_tiles
        m = jnp.where(is_lat, mod_ref[0, 1], mod_ref[0, 0])
        merged = (gate_ref[0, rows, 0:d].astype(F32) * _dot(om_ref[0, rows, :], wb_ref[0])
                  + gate_ref[0, rows, d:2 * d].astype(F32) * _dot(op_ref[0, rows, :], wb_ref[1])
                  + gate_ref[0, rows, 2 * d:3 * d].astype(F32) * _dot(od_ref[0, rows, :], wb_ref[2]))
        y = _dot(merged.astype(BF16), wo_ref[...])
        h2 = h_ref[0, rows, :] + m[2:3] * _rms(y, gpost_ref[...])
        h2_ref[0, rows, :] = h2
        fx = _rms(h2, gffn_ref[...]) * (1.0 + m[4:5]) + m[3:4]
        fhi, flo = _split_bf16(fx)
        fx_ref[0, rows, :] = fhi
        lg_ref[:, rows] = _dot_nt(whi, fhi) + _dot_nt(whi, flo) + _dot_nt(wlo, fhi)


def _merge(o_mla, o_diff, o_pool, gates, h, mods, wb, w_out, g_post, g_ffn, w_rt, n_lat, nt, groups):
    b, _, d = h.shape
    ne = w_rt.shape[0]
    assert nt % groups == 0
    steps = nt // groups
    tg = groups * TM
    tile = lambda w: pl.BlockSpec((1, tg, w), lambda bi, i: (bi, i, 0))
    return pl.pallas_call(
        functools.partial(_merge_kernel, n_lat_tiles=n_lat // TM, groups=groups),
        grid=(b, steps),
        in_specs=[
            tile(o_mla.shape[-1]), tile(o_diff.shape[-1]), tile(o_pool.shape[-1]), tile(gates.shape[-1]), tile(d),
            pl.BlockSpec((1, 2, 6, d), lambda bi, i: (bi, 0, 0, 0)),
            _const_spec(wb.shape), _const_spec(w_out.shape), _const_spec(g_post.shape), _const_spec(g_ffn.shape),
            _const_spec(w_rt.shape),
        ],
        out_specs=[tile(d), tile(d), pl.BlockSpec((ne, tg), lambda bi, i: (0, bi * steps + i))],
        out_shape=[jax.ShapeDtypeStruct((b, nt * TM, d), F32), jax.ShapeDtypeStruct((b, nt * TM, d), BF16),
                   jax.ShapeDtypeStruct((ne, b * nt * TM), F32)],
        compiler_params=pltpu.CompilerParams(
            dimension_semantics=("parallel", "parallel"), vmem_limit_bytes=VMEM_LIMIT),
        name="merge",
    )(o_mla, o_diff, o_pool, gates, h, mods, wb, w_out, g_post, g_ffn, w_rt)


def _route_kernel(lg_ref, br_ref, posm_ref, wts_ref, meta_ref, tot_ref, carry_ref):
    ne, tm = lg_ref.shape
    gsz = ne // N_GROUPS
    neg = -jnp.inf

    @pl.when(pl.program_id(0) == 0)
    def _():
        carry_ref[...] = jnp.zeros_like(carry_ref)

    scores = jax.nn.sigmoid(lg_ref[...])
    sel = scores + br_ref[...]

    jidx = lax.broadcasted_iota(I32, (gsz, tm), 0).astype(F32)
    gscore = []
    for g in range(N_GROUPS):
        v = sel[g * gsz:(g + 1) * gsz]
        m1 = jnp.max(v, axis=0, keepdims=True)
        first = jnp.min(jnp.where(v == m1, jidx, float(gsz)), axis=0, keepdims=True)
        m2 = jnp.max(jnp.where(jidx == first, neg, v), axis=0, keepdims=True)
        gscore.append(m1 + m2)
    masked = []
    for g in range(N_GROUPS):
        rank = jnp.zeros((1, tm), I32)
        for o in range(N_GROUPS):
            if o == g:
                continue
            ahead = gscore[o] > gscore[g]
            if o < g:
                ahead = ahead | (gscore[o] == gscore[g])
            rank = rank + ahead.astype(I32)
        masked.append(jnp.where(rank < TOPK_GROUPS, sel[g * gsz:(g + 1) * gsz], neg))
    cur = jnp.concatenate(masked, axis=0)

    eidx = lax.broadcasted_iota(I32, (ne, tm), 0).astype(F32)
    chosen = jnp.zeros((ne, tm), jnp.bool_)
    for _ in range(TOP_K):
        mx = jnp.max(cur, axis=0, keepdims=True)
        first = jnp.min(jnp.where(cur == mx, eidx, float(ne)), axis=0, keepdims=True)
        hit = eidx == first
        chosen = chosen | hit
        cur = jnp.where(hit, neg, cur)

    wsel = jnp.where(chosen, scores, 0.0)
    wts_ref[...] = wsel / jnp.sum(wsel, axis=0, keepdims=True) * ROUTED_SCALE

    cmask = chosen.astype(BF16)
    ti = lax.broadcasted_iota(I32, (tm, tm), 0)
    tj = lax.broadcasted_iota(I32, (tm, tm), 1)
    pos = _dot(cmask, (ti < tj).astype(BF16))
    posm_ref[...] = jnp.where(chosen, pos, -1.0)
    ei = lax.broadcasted_iota(I32, (ne, ne), 0)
    ej = lax.broadcasted_iota(I32, (ne, ne), 1)
    cnt = jnp.sum(chosen.astype(F32), axis=1, keepdims=True)
    cpad = jnp.floor((cnt + (ALIGN - 1)) * (1.0 / ALIGN)) * ALIGN
    cpad_b = jnp.broadcast_to(cpad, (ne, LANES))
    loc_b = _dot((ej < ei).astype(BF16), cpad_b.astype(BF16))

    gpos = carry_ref[...]
    lane = lax.broadcasted_iota(I32, (ne, LANES), 1)
    meta_ref[0] = jnp.where(lane == 0, cpad_b, jnp.where(lane == 1, loc_b, gpos)).astype(I32)
    carry_ref[...] = gpos + cpad_b
    tot_ref[...] = (gpos + cpad_b).astype(I32)


def _route(logits_t, b_router):
    ne, ntok = logits_t.shape
    ntiles = ntok // TM
    return pl.pallas_call(
        _route_kernel,
        grid=(ntiles,),
        in_specs=[pl.BlockSpec((ne, TM), lambda t: (0, t)), pl.BlockSpec((ne, 1), lambda t: (0, 0))],
        out_specs=[
            pl.BlockSpec((ne, TM), lambda t: (0, t)),
            pl.BlockSpec((ne, TM), lambda t: (0, t)),
            pl.BlockSpec((1, ne, LANES), lambda t: (t, 0, 0)),
            pl.BlockSpec((ne, LANES), lambda t: (0, 0)),
        ],
        out_shape=[
            jax.ShapeDtypeStruct((ne, ntok), F32),
            jax.ShapeDtypeStruct((ne, ntok), F32),
            jax.ShapeDtypeStruct((ntiles, ne, LANES), I32),
            jax.ShapeDtypeStruct((ne, LANES), I32),
        ],
        scratch_shapes=[pltpu.VMEM((ne, LANES), F32)],
        compiler_params=pltpu.CompilerParams(dimension_semantics=("arbitrary",)),
        name="route",
    )(logits_t, b_router)


WAIT_ROWS = 16 * ALIGN
START_UNROLL = 4


def _copy(local, remote, l0, r0, rows, sem, to_remote):
    lrows = local.at[pl.ds(l0, rows)]
    rrows = remote.at[pl.ds(r0, rows)]
    return pltpu.make_async_copy(lrows, rrows, sem) if to_remote else pltpu.make_async_copy(rrows, lrows, sem)


def _start_chunks(rem_ref, n, local, remote, sem, to_remote):
    def start(c):
        _copy(local, remote, pl.multiple_of(c * ALIGN, ALIGN), pl.multiple_of(rem_ref[0, 0, c], ALIGN), ALIGN,
              sem, to_remote).start()

    def per_group(g, carry):
        for u in range(START_UNROLL):
            start(g * START_UNROLL + u)
        return carry

    groups = n // START_UNROLL
    lax.fori_loop(0, groups, per_group, 0)
    lax.fori_loop(groups * START_UNROLL, n, lambda c, carry: (start(c), carry)[1], 0)


def _wait_chunks(n, local, remote, sem, to_remote):
    per = WAIT_ROWS // ALIGN
    lax.fori_loop(0, n // per, lambda i, c: (_copy(local, remote, 0, 0, WAIT_ROWS, sem, to_remote).wait(), c)[1], 0)
    lax.fori_loop(0, lax.rem(n, per), lambda i, c: (_copy(local, remote, 0, 0, ALIGN, sem, to_remote).wait(), c)[1], 0)


def _dispatch_kernel(n16_ref, tail_ref, tailn_ref, nused_ref, fx_ref, posm_ref, lohi_ref, rem_ref, xs_ref,
                     buf_ref, zero_ref, sem):
    t = pl.program_id(0)
    slot = lax.rem(t, 2)
    x = fx_ref[0]
    buf = buf_ref.at[slot]
    posm = posm_ref[...]
    m2 = jnp.concatenate([posm, jnp.zeros((LANES - posm.shape[0], posm.shape[1]), F32)], axis=0).astype(BF16)
    lo = lohi_ref[0, 0:1, :]
    hi = lohi_ref[0, 1:2, :]
    def sort_chunk(rc):
        r = (lax.broadcasted_iota(I32, (RCH, 1), 0) + rc * RCH).astype(F32)
        member = (r >= lo) & (r < hi)
        base = jnp.sum(jnp.where(member, lo, 0.0), axis=1, keepdims=True)
        onehot = (_dot(member.astype(BF16), m2) == r - base).astype(BF16)
        buf[rc * RCH:(rc + 1) * RCH, :] = _dot(onehot, x).astype(buf_ref.dtype)

    last_rc = RL // RCH - 1
    for rc in range(last_rc):
        sort_chunk(rc)
    pl.when(n16_ref[t] * ALIGN > last_rc * RCH)(lambda: sort_chunk(last_rc))

    @pl.when(t > 0)
    def _():
        _wait_chunks(n16_ref[t - 1], buf_ref.at[1 - slot], xs_ref, sem.at[0], True)

    _start_chunks(rem_ref, n16_ref[t], buf, xs_ref, sem.at[0], True)

    def tail(act):
        def go(cp):
            if act == 0:
                cp.start()
            else:
                cp.wait()

        def per_expert(e, carry):
            s0 = tail_ref[e]

            def per_chunk(j, c):
                go(pltpu.make_async_copy(
                    zero_ref.at[pl.ds(0, ALIGN)],
                    xs_ref.at[pl.ds(pl.multiple_of(s0 + j * ALIGN, ALIGN), ALIGN)], sem.at[1]))
                return c

            lax.fori_loop(0, tailn_ref[e], per_chunk, 0)
            return carry

        lax.fori_loop(0, N_EXPERTS, per_expert, 0)

        def per_block(j, c):
            go(pltpu.make_async_copy(zero_ref, xs_ref.at[pl.ds(pl.multiple_of(j * EBLK, EBLK), EBLK)], sem.at[1]))
            return c

        lax.fori_loop(nused_ref[0], xs_ref.shape[0] // EBLK, per_block, 0)

    @pl.when(t == pl.num_programs(0) - 1)
    def _():
        zero_ref[...] = jnp.zeros_like(zero_ref)
        tail(0)
        _wait_chunks(n16_ref[t], buf, xs_ref, sem.at[0], True)
        tail(1)


def _dispatch(fx, posm, lohi, tabs, rmax):
    b, rows, d = fx.shape
    nt = rows // TM
    ntiles = b * nt
    rem = tabs["rem"]
    return pl.pallas_call(
        _dispatch_kernel,
        grid_spec=pltpu.PrefetchScalarGridSpec(
            num_scalar_prefetch=4,
            grid=(ntiles,),
            in_specs=[
                pl.BlockSpec((1, TM, d), lambda t, *_: (t // nt, t % nt, 0)),
                pl.BlockSpec((posm.shape[0], TM), lambda t, *_: (0, t)),
                pl.BlockSpec((1,) + lohi.shape[1:], lambda t, *_: (t, 0, 0)),
                pl.BlockSpec((1,) + rem.shape[1:], lambda t, *_: (t, 0, 0), memory_space=pltpu.SMEM),
            ],
            out_specs=pl.BlockSpec(memory_space=pl.ANY),
            scratch_shapes=[pltpu.VMEM((2, RL, d), BF16), pltpu.VMEM((EBLK, d), BF16),
                            pltpu.SemaphoreType.DMA((2,))],
        ),
        out_shape=jax.ShapeDtypeStruct((rmax, d), BF16),
        compiler_params=pltpu.CompilerParams(dimension_semantics=("arbitrary",), vmem_limit_bytes=VMEM_LIMIT),
        name="dispatch",
    )(tabs["n16"], tabs["tail_start"], tabs["tail_n16"], tabs["n_used"], fx, posm, lohi, rem)


def _expert_kernel(blk_ref, nused_ref, x_ref, wg_ref, wu_ref, wd_ref, y_ref, wgu_bf, wd_bf):
    i = pl.program_id(0)

    @pl.when((i == 0) | (blk_ref[i] != blk_ref[jnp.maximum(i - 1, 0)]))
    def _():
        wgu_bf[:, :EXPERT_FF] = wg_ref[0].astype(BF16)
        wgu_bf[:, EXPERT_FF:] = wu_ref[0].astype(BF16)
        wd_bf[...] = wd_ref[0].astype(BF16)

    @pl.when(i < nused_ref[0])
    def _():
        gu = _dot(x_ref[...], wgu_bf[...])
        hmid = (_silu(gu[:, :EXPERT_FF]) * gu[:, EXPERT_FF:]).astype(BF16)
        y_ref[...] = _dot(hmid, wd_bf[...]).astype(y_ref.dtype)


def _experts(xs, blk_e, n_used, w_g, w_u, w_d, layer):
    rmax, d = xs.shape
    nblk = rmax // EBLK
    row = lambda i, blk, nu: (jnp.minimum(i, nu[0] - 1), 0)
    wsel = lambda i, blk, nu: (layer, blk[i], 0, 0)
    return pl.pallas_call(
        _expert_kernel,
        grid_spec=pltpu.PrefetchScalarGridSpec(
            num_scalar_prefetch=2,
            grid=(nblk,),
            in_specs=[
                pl.BlockSpec((EBLK, d), row),
                pl.BlockSpec((None, 1, d, EXPERT_FF), wsel),
                pl.BlockSpec((None, 1, d, EXPERT_FF), wsel),
                pl.BlockSpec((None, 1, EXPERT_FF, d), wsel),
            ],
            out_specs=pl.BlockSpec((EBLK, d), row),
            scratch_shapes=[pltpu.VMEM((d, 2 * EXPERT_FF), BF16), pltpu.VMEM((EXPERT_FF, d), BF16)],
        ),
        out_shape=jax.ShapeDtypeStruct((rmax, d), BF16),
        input_output_aliases={2: 0},
        compiler_params=pltpu.CompilerParams(dimension_semantics=("arbitrary",), vmem_limit_bytes=VMEM_LIMIT),
        name="experts",
    )(blk_e, n_used, xs, w_g, w_u, w_d)


def _combine_kernel(n16_ref, y_ref, posmt_ref, wt_ref, meta_ref, fx_ref, h_ref, mod_ref, wsgu_ref,
                    wsd_ref, gpost_ref, rem_ref, remn_ref, o_ref, buf_ref, sem):
    t = pl.program_id(0)
    slot = lax.rem(t, 2)
    buf = buf_ref.at[slot]

    @pl.when(t == 0)
    def _():
        buf_ref[...] = jnp.zeros_like(buf_ref)
        _start_chunks(rem_ref, n16_ref[t], buf, y_ref, sem.at[slot], False)

    @pl.when(t + 1 < pl.num_programs(0))
    def _():
        _start_chunks(remn_ref, n16_ref[t + 1], buf_ref.at[1 - slot], y_ref, sem.at[1 - slot], False)

    x = fx_ref[0]
    gu = _dot(x, wsgu_ref[...])
    ff = wsd_ref.shape[0]
    o_ref[0] = _dot((_silu(gu[:, :ff]) * gu[:, ff:]).astype(BF16), wsd_ref[...])

    _wait_chunks(n16_ref[t], buf, y_ref, sem.at[slot], False)

    posmt = posmt_ref[...].astype(BF16)
    wt = wt_ref[...].astype(BF16)
    lo = meta_ref[0, :, 1:2].astype(F32)
    hi = lo + meta_ref[0, :, 0:1].astype(F32)
    def unsort_chunk(rc):
        r = (lax.broadcasted_iota(I32, (1, RCH), 1) + rc * RCH).astype(F32)
        member = (r >= lo) & (r < hi)
        base = jnp.sum(jnp.where(member, lo, 0.0), axis=0, keepdims=True)
        mb = member.astype(BF16)
        pw = jnp.where(_dot(posmt, mb) == r - base, _dot(wt, mb), 0.0).astype(BF16)
        o_ref[0] += _dot(pw, buf[rc * RCH:(rc + 1) * RCH, :])

    last_rc = RL // RCH - 1
    for rc in range(last_rc):
        unsort_chunk(rc)
    pl.when(n16_ref[t] * ALIGN > last_rc * RCH)(lambda: unsort_chunk(last_rc))

    m = mod_ref[0, 0]
    o_ref[0] = h_ref[0] + m[5:6] * _rms(o_ref[0], gpost_ref[...])


def _combine(y, posm_t, wts_t, meta, fx, h2, mods, tabs, w_sgu, w_sd, g_post, n_lat):
    b, rows, d = fx.shape
    nt = rows // TM
    ntiles = b * nt
    nlt = n_lat // TM
    ne = posm_t.shape[1]
    rem = tabs["rem"]
    tile = lambda t, *_: (t // nt, t % nt, 0)
    return pl.pallas_call(
        _combine_kernel,
        grid_spec=pltpu.PrefetchScalarGridSpec(
            num_scalar_prefetch=1,
            grid=(ntiles,),
            in_specs=[
                pl.BlockSpec(memory_space=pl.ANY),
                pl.BlockSpec((TM, ne), lambda t, *_: (t, 0)),
                pl.BlockSpec((TM, ne), lambda t, *_: (t, 0)),
                pl.BlockSpec((1,) + meta.shape[1:], lambda t, *_: (t, 0, 0)),
                pl.BlockSpec((1, TM, d), tile),
                pl.BlockSpec((1, TM, d), tile),
                pl.BlockSpec((1, 1, 6, d), lambda t, *_: (t // nt, jnp.where(t % nt < nlt, 1, 0), 0, 0)),
                pl.BlockSpec(w_sgu.shape, lambda t, *_: (0, 0), pipeline_mode=pl.Buffered(1)),
                pl.BlockSpec(w_sd.shape, lambda t, *_: (0, 0), pipeline_mode=pl.Buffered(1)),
                pl.BlockSpec(g_post.shape, lambda t, *_: (0, 0), pipeline_mode=pl.Buffered(1)),
                pl.BlockSpec((1,) + rem.shape[1:], lambda t, *_: (t, 0, 0), memory_space=pltpu.SMEM),
                pl.BlockSpec((1,) + rem.shape[1:], lambda t, *_: (jnp.minimum(t + 1, ntiles - 1), 0, 0),
                             memory_space=pltpu.SMEM),
            ],
            out_specs=pl.BlockSpec((1, TM, d), tile),
            scratch_shapes=[pltpu.VMEM((2, RL, d), BF16), pltpu.SemaphoreType.DMA((2,))],
        ),
        out_shape=jax.ShapeDtypeStruct((b, rows, d), F32),
        compiler_params=pltpu.CompilerParams(dimension_semantics=("arbitrary",), vmem_limit_bytes=VMEM_LIMIT),
        name="combine",
    )(tabs["n16"], y, posm_t, wts_t, meta, fx, h2, mods, w_sgu, w_sd, g_post, rem, rem)


def _rope_tables(n_lat, n_ctx, rot_dim, lane_slots):
    rows = n_lat // GRID_W
    pos_row = jnp.repeat(jnp.arange(rows, dtype=F32), GRID_W)
    pos_col = jnp.tile(jnp.arange(GRID_W, dtype=F32), rows)
    d_axis = rot_dim // 2
    inv_freq = ROPE_BASE ** (-jnp.arange(0, d_axis, 2, dtype=F32) / d_axis)
    ar = pos_row[:, None] * inv_freq
    ac = pos_col[:, None] * inv_freq
    ang = jnp.concatenate([ar, ar, ac, ac], axis=1)
    qtr = rot_dim // 4
    first = (jnp.arange(rot_dim) // qtr) % 2 == 0
    cos, sin = jnp.cos(ang), jnp.sin(ang)
    s1 = jnp.where(first, -sin, 0.0)
    s2 = jnp.where(first, 0.0, sin)
    lc = n_lat + n_ctx
    c_t = jnp.ones((lc, LANES), F32)
    s1_t = jnp.zeros((lc, LANES), F32)
    s2_t = jnp.zeros((lc, LANES), F32)
    for lo in lane_slots:
        c_t = c_t.at[:n_lat, lo:lo + rot_dim].set(cos)
        s1_t = s1_t.at[:n_lat, lo:lo + rot_dim].set(s1)
        s2_t = s2_t.at[:n_lat, lo:lo + rot_dim].set(s2)
    return jnp.stack([c_t, s1_t, s2_t])


def _pack_layer_weights(w_in, w_uq, w_ukv):
    d = w_in.shape[0]
    kv0 = 0
    kr0 = kv0 + MLA_KV_RANK
    kd0 = kr0 + MLA_ROPE
    vd0 = kd0 + 512
    q0 = vd0 + 512
    qd0 = q0 + MLA_Q_RANK
    p0 = qd0 + 512
    g0 = p0 + 512
    kr = jnp.zeros((d, LANES), w_in.dtype).at[:, MLA_NOPE:MLA_NOPE + MLA_ROPE].set(w_in[:, kr0:kd0])
    w1 = jnp.concatenate([w_in[:, kv0:kr0], kr, w_in[:, kd0:vd0], w_in[:, vd0:q0], w_in[:, q0:qd0],
                          w_in[:, qd0:p0], w_in[:, p0:g0], w_in[:, g0:]], axis=1).astype(BF16)
    ukv = w_ukv.reshape(MLA_KV_RANK, MLA_HEADS, MLA_NOPE + MLA_V)
    uk = jnp.pad(ukv[:, :, :MLA_NOPE], ((0, 0), (0, 0), (0, LANES - MLA_NOPE))).reshape(MLA_KV_RANK, -1)
    uv = ukv[:, :, MLA_NOPE:].reshape(MLA_KV_RANK, -1)
    w_ukv_p = jnp.concatenate([uk, uv], axis=1).astype(BF16)
    uq = w_uq.reshape(MLA_Q_RANK, MLA_HEADS, MLA_NOPE + MLA_ROPE)
    w_uq_p = jnp.pad(uq, ((0, 0), (0, 0), (0, LANES - MLA_NOPE - MLA_ROPE))).reshape(MLA_Q_RANK, -1).astype(BF16)
    return w1, w_ukv_p, w_uq_p


def _routing_tables(meta, totals, rmax):
    cpad, loc, gpos = meta[:, :, 0], meta[:, :, 1], meta[:, :, 2]
    tot = totals[:, 0]
    reg = (tot + EBLK - 1) // EBLK * EBLK
    reg_end = jnp.cumsum(reg)
    reg_start = reg_end - reg
    n_used = jnp.maximum(reg_end[-1:] // EBLK, 1).astype(I32)
    blk_start = jnp.arange(rmax // EBLK, dtype=I32) * EBLK
    blk_e = jnp.minimum(jnp.sum(reg_end[None, :] <= blk_start[:, None], axis=1), N_EXPERTS - 1).astype(I32)
    ntiles, ne = loc.shape
    lohi = jnp.zeros((ntiles, 8, LANES), F32).at[:, 0, :].set(float(RL)).at[:, 0, :ne].set(loc.astype(F32))
    lohi = lohi.at[:, 1, :ne].set((loc + cpad).astype(F32))
    chunk = jnp.arange(RL // ALIGN, dtype=I32)[None, :, None]
    seg_lo = (loc // ALIGN)[:, None, :]
    seg_end = (loc + cpad) // ALIGN
    inside = (seg_lo <= chunk) & (chunk < seg_end[:, None, :])
    shift = jnp.sum(jnp.where(inside, (reg_start[None, :] + gpos - loc)[:, None, :], 0), axis=-1)
    rem = (shift + chunk[:, :, 0] * ALIGN).astype(I32)[:, None, :]
    return dict(
        n16=seg_end[:, -1].astype(I32), rem=rem,
        tail_start=(reg_start + tot).astype(I32), tail_n16=((reg - tot) // ALIGN).astype(I32),
        n_used=n_used), blk_e, lohi


def kernel(x, c, ctx, c_ctx, w_mod, b_mod, g_pre_mix, g_post_mix, g_pre_ffn, g_post_ffn, w_in, g_cq, w_uq, g_ckv, w_ukv, w_pool, pool_scale, lam_q1, lam_k1, lam_q2, lam_k2, g_subln, w_b_mla, w_b_pool, w_b_diff, w_out, w_router, b_router, w_e_gate, w_e_up, w_e_down, w_s_gate, w_s_up, w_s_down):
    b, n_lat, d = x.shape
    n_ctx = ctx.shape[1]
    depth = w_mod.shape[0]
    assert n_ctx == TM and n_lat % TQ == 0 and n_lat % GRID_W == 0
    lc = n_lat + n_ctx
    row2 = lambda v: v.reshape(1, -1)

    rows = (b + 1 + 7) // 8 * 8
    cvec = jnp.zeros((rows, d), F32).at[:b].set(c).at[b].set(c_ctx)
    mod_all = _modulation(cvec, w_mod, b_mod).reshape(depth, rows, 6, d)

    tab_m = _rope_tables(n_lat, n_ctx, MLA_ROPE, (MLA_NOPE,))
    tab_d = _rope_tables(n_lat, n_ctx, DIFF_HALF, (0, DIFF_HALF))

    h = jnp.concatenate([x, ctx], axis=1)
    for l in range(depth):
        last = l == depth - 1
        lam_init = 0.8 - 0.6 * math.exp(-0.3 * l)
        mods = jnp.stack([jnp.broadcast_to(mod_all[l, b], (b, 6, d)), mod_all[l, :b]], axis=1)
        w1, w_ukv_p, w_uq_p = _pack_layer_weights(w_in[l], w_uq[l], w_ukv[l])

        qm, km, vm, qd, kd, vd, pool_in, gates = _project(
            h, mods, row2(g_pre_mix[l]), w1, row2(g_ckv[l]), w_ukv_p, row2(g_cq[l]), w_uq_p, tab_m, tab_d, n_lat)
        o_mla, o_mla_ctx = _attention(
            functools.partial(_mla_kernel, n_lat=n_lat, has_ctx=not last), "mla_attention", qm, km, vm, (),
            [pltpu.VMEM((2, lc, 2 * MLA_V), BF16)], n_lat, not last, MLA_HEADS // 2, 2 * LANES, 2 * MLA_V, 2 * MLA_V)
        lam_p = jnp.stack([lam_q1[l], lam_k1[l], lam_q2[l], lam_k2[l]])
        o_diff, o_diff_ctx = _attention(
            functools.partial(_diff_kernel, n_lat=n_lat, has_ctx=not last, lam_init=lam_init), "diff_attention",
            qd, kd, vd, (lam_p, row2(g_subln[l])), [], n_lat, not last, DIFF_HEADS, LANES, DIFF_V, DIFF_V)
        segments = ((0, n_lat),) if last else ((0, n_lat), (n_lat, n_ctx))
        o_pool = _pool_mix(pool_in, w_pool[l].astype(BF16), row2(pool_scale[l]), segments)

        nt = (n_lat if last else lc) // TM
        if not last:
            o_mla = jnp.concatenate([o_mla, o_mla_ctx], axis=1)
            o_diff = jnp.concatenate([o_diff, o_diff_ctx], axis=1)
        groups = next(g for g in (4, 3, 2, 1) if nt % g == 0)
        wb = jnp.stack([w_b_mla[l], w_b_pool[l], w_b_diff[l]]).astype(BF16)
        h2, fx, logits_t = _merge(o_mla, o_diff, o_pool, gates, h, mods, wb, w_out[l].astype(BF16),
                                  row2(g_post_mix[l]), row2(g_pre_ffn[l]), w_router[l].T, n_lat, nt, groups)

        posm, wts, meta, totals = _route(logits_t, b_router[l].reshape(-1, 1))
        rmax = meta.shape[0] * RL + N_EXPERTS * EBLK
        tabs, blk_e, lohi = _routing_tables(meta, totals, rmax)
        xs = _dispatch(fx, posm, lohi, tabs, rmax)
        ys = _experts(xs, blk_e, tabs["n_used"], w_e_gate, w_e_up, w_e_down, l)
        w_sgu = jnp.concatenate([w_s_gate[l], w_s_up[l]], axis=-1).astype(BF16)
        h = _combine(ys, posm.T, wts.T, meta, fx, h2, mods, tabs, w_sgu, w_s_down[l].astype(BF16),
                     row2(g_post_ffn[l]), n_lat)
    return h
```

```python
import functools
import math

import jax
import jax.numpy as jnp
from jax import lax
from jax.experimental import pallas as pl
from jax.experimental.pallas import tpu as pltpu

F32 = jnp.float32
BF16 = jnp.bfloat16
I32 = jnp.int32

EPS = 1e-6
ROPE_BASE = 10000.0
GRID_W = 64

MLA_HEADS = 8
MLA_NOPE = 64
MLA_ROPE = 32
MLA_V = 64
MLA_Q_RANK = 384
MLA_KV_RANK = 256
POOL_WINDOWS = (2, 4, 8, 16)
POOL_GROUP_W = 128
DIFF_HEADS = 4
DIFF_HALF = 64
DIFF_V = 128
N_EXPERTS = 64
TOP_K = 8
N_GROUPS = 8
TOPK_GROUPS = 4
EXPERT_FF = 256
ROUTED_SCALE = 2.5

LANES = 128
TM = 256
QROWS = 256
ALIGN = 16
EBLK = 1024
RL = TM * TOP_K + N_EXPERTS * ALIGN
RCH = 512
VMEM_LIMIT = 56 * 1024 * 1024
LOG2E = 1.4426950408889634

C_CKV = 0
C_KR = C_CKV + MLA_KV_RANK
C_KD = C_KR + LANES
C_VD = C_KD + 512
C_CQ = C_VD + 512
C_QD = C_CQ + MLA_Q_RANK
C_POOL = C_QD + 512
C_GATE = C_POOL + 512


def _rms(x, g):
    ms = jnp.mean(x * x, axis=-1, keepdims=True)
    return x * lax.rsqrt(ms + EPS) * g


def _dot(a, b):
    return jnp.dot(a, b, preferred_element_type=F32)


def _dot_nt(a, b):
    return lax.dot_general(a, b, (((1,), (1,)), ((), ())), preferred_element_type=F32)


def _split_bf16(x):
    hi = x.astype(BF16)
    lo = (x - hi.astype(F32)).astype(BF16)
    return hi, lo


def _silu(x):
    return x * jax.nn.sigmoid(x)


def _rope(x, tab, quarter):
    w = x.shape[-1]
    rep = w // LANES
    c, s1, s2 = (jnp.tile(t, (1, rep)) if rep > 1 else t for t in tab)
    return x * c + pltpu.roll(x, w - quarter, 1) * s1 + pltpu.roll(x, quarter, 1) * s2


def _const_spec(shape):
    nd = len(shape)
    return pl.BlockSpec(shape, lambda *_: (0,) * nd, pipeline_mode=pl.Buffered(1))


def _mod_kernel(a_ref, w_ref, b_ref, o_ref):
    a = _silu(a_ref[...])
    ahi, alo = _split_bf16(a)
    whi, wlo = _split_bf16(w_ref[...])
    o_ref[...] = _dot(ahi, whi) + _dot(alo, whi) + _dot(ahi, wlo) + b_ref[...]


def _modulation(cvec, w_mod, b_mod):
    nl, d, d6 = w_mod.shape
    rows = cvec.shape[0]
    nc = d6 // d
    return pl.pallas_call(
        _mod_kernel,
        grid=(nl, nc),
        in_specs=[
            pl.BlockSpec((rows, d), lambda l, j: (0, 0)),
            pl.BlockSpec((None, d, d), lambda l, j: (l, 0, j)),
            pl.BlockSpec((None, 1, d), lambda l, j: (l, 0, j)),
        ],
        out_specs=pl.BlockSpec((None, rows, d), lambda l, j: (l, 0, j)),
        out_shape=jax.ShapeDtypeStruct((nl, rows, d6), F32),
        name="modulation",
    )(cvec, w_mod, b_mod.reshape(nl, 1, d6))


PROJ_GROUP = 2


def _proj_kernel(h_ref, mod_ref, gpre_ref, w1_ref, gckv_ref, wukv_ref, gcq_ref, wuq_ref, tm_ref, td_ref,
                 qm_ref, km_ref, vm_ref, qd_ref, kd_ref, vd_ref, pool_ref, gate_ref, *, tiles_per_sample,
                 n_lat_tiles, groups):
    d = h_ref.shape[-1]
    hx, tabm, tabd = [], [], []
    for g in range(groups):
        tile = pl.program_id(0) * groups + g
        sample = tile // tiles_per_sample
        pos_tile = tile - sample * tiles_per_sample
        m = jnp.where(pos_tile < n_lat_tiles, mod_ref[sample, 1], mod_ref[sample, 0])
        pos = pl.ds(pl.multiple_of(pos_tile * TM, TM), TM)
        tabm.append([tm_ref[k, pos, :] for k in range(3)])
        tabd.append([td_ref[k, pos, :] for k in range(3)])
        hx.append((_rms(h_ref[g * TM:(g + 1) * TM, :], gpre_ref[...]) * (1.0 + m[1:2]) + m[0:1]).astype(BF16))
    hx = jnp.concatenate(hx, axis=0)
    tabm = tuple(jnp.concatenate([tg[k] for tg in tabm], axis=0) for k in range(3))
    tabd = tuple(jnp.concatenate([tg[k] for tg in tabd], axis=0) for k in range(3))

    ckv = _dot(hx, w1_ref[:, C_CKV:C_KR])
    kv = _dot(_rms(ckv, gckv_ref[...]).astype(BF16), wukv_ref[...])
    kr = _rope(_dot(hx, w1_ref[:, C_KR:C_KD]), tabm, MLA_ROPE // 4)
    nk = MLA_HEADS * LANES
    km_ref[...] = (kv[:, :nk] + jnp.tile(kr, (1, MLA_HEADS))).astype(BF16)
    vm_ref[...] = kv[:, nk:].astype(BF16)

    kd_ref[...] = _rope(_dot(hx, w1_ref[:, C_KD:C_VD]), tabd, DIFF_HALF // 4).astype(BF16)
    vd_ref[...] = _dot(hx, w1_ref[:, C_VD:C_CQ]).astype(BF16)

    cq = _dot(hx, w1_ref[:, C_CQ:C_QD])
    q = _dot(_rms(cq, gcq_ref[...]).astype(BF16), wuq_ref[...])
    qm_scale = LOG2E / math.sqrt(MLA_NOPE + MLA_ROPE)
    qm_ref[...] = (_rope(q, tabm, MLA_ROPE // 4) * qm_scale).astype(BF16)
    qd_scale = LOG2E / math.sqrt(DIFF_HALF)
    qd_ref[...] = (_rope(_dot(hx, w1_ref[:, C_QD:C_POOL]), tabd, DIFF_HALF // 4) * qd_scale).astype(BF16)
    pool_ref[...] = _dot(hx, w1_ref[:, C_POOL:C_GATE]).astype(BF16)
    for j in range(3):
        lo = C_GATE + j * d
        gate_ref[:, j * d:(j + 1) * d] = jax.nn.sigmoid(_dot(hx, w1_ref[:, lo:lo + d])).astype(BF16)


def _project(h, mods, g_pre, w1, g_ckv, w_ukv, g_cq, w_uq, tab_m, tab_d, n_lat):
    b, lc, d = h.shape
    nt = lc // TM
    groups = PROJ_GROUP if (b * nt) % PROJ_GROUP == 0 else 1
    tile = lambda w: pl.BlockSpec((groups * TM, w), lambda i: (i, 0))
    widths = (MLA_HEADS * LANES, MLA_HEADS * LANES, MLA_HEADS * MLA_V, 512, 512, 512, 512, 3 * d)
    outs = pl.pallas_call(
        functools.partial(_proj_kernel, tiles_per_sample=nt, n_lat_tiles=n_lat // TM, groups=groups),
        grid=(b * nt // groups,),
        in_specs=[tile(d)] + [_const_spec(a.shape) for a in (mods, g_pre, w1, g_ckv, w_ukv, g_cq, w_uq, tab_m, tab_d)],
        out_specs=[tile(w) for w in widths],
        out_shape=[jax.ShapeDtypeStruct((b * lc, w), BF16) for w in widths],
        compiler_params=pltpu.CompilerParams(dimension_semantics=("parallel",), vmem_limit_bytes=VMEM_LIMIT),
        name="project",
    )(h.reshape(b * lc, d), mods, g_pre, w1, g_ckv, w_ukv, g_cq, w_uq, tab_m, tab_d)
    return [o.reshape(b, lc, -1) for o in outs]


def _exp_scores(q, k):
    s = _dot_nt(q, k)
    return jnp.exp2(s - jnp.max(s, axis=-1, keepdims=True))


def _query_groups(n_lat, lc, has_ctx):
    groups = [(r0, QROWS, 0) for r0 in range(0, n_lat, QROWS)]
    return groups + ([(n_lat, lc - n_lat, n_lat)] if has_ctx else [])


def _mla_kernel(q_ref, k_ref, v_ref, o_ref, vext_ref, *, n_lat, has_ctx):
    lc = k_ref.shape[1]
    lane = lax.broadcasted_iota(I32, (1, LANES), 1)
    v = v_ref[0]
    one = jnp.ones_like(v)
    vext_ref[0] = jnp.where(lane < MLA_V, v, one)
    vext_ref[1] = jnp.where(lane < MLA_V, one, v)
    for r0, rows, k0 in _query_groups(n_lat, lc, has_ctx):
        pv = []
        for j in range(2):
            e = _exp_scores(q_ref[0, r0:r0 + rows, j * LANES:(j + 1) * LANES],
                            k_ref[0, k0:lc, j * LANES:(j + 1) * LANES])
            pv.append(_dot(e.astype(BF16), vext_ref[j, k0:lc, :]))
        o0 = pv[0] * (1.0 / pv[0][:, MLA_V:MLA_V + 1])
        o1 = pv[1] * (1.0 / pv[1][:, 0:1])
        o_ref[0, r0:r0 + rows, :] = jnp.where(lane < MLA_V, o0, o1).astype(o_ref.dtype)


def _diff_kernel(q_ref, k_ref, v_ref, lam_ref, g_ref, o_ref, *, n_lat, has_ctx, lam_init):
    lc = k_ref.shape[1]
    lane = lax.broadcasted_iota(I32, (1, LANES), 1)
    lp = lam_ref[...]
    lam = (jnp.exp(jnp.sum(lp[0:1] * lp[1:2], axis=-1, keepdims=True))
           - jnp.exp(jnp.sum(lp[2:3] * lp[3:4], axis=-1, keepdims=True)) + lam_init)
    for r0, rows, k0 in _query_groups(n_lat, lc, has_ctx):
        q = q_ref[0, r0:r0 + rows, :]
        k = k_ref[0, k0:lc, :]
        v = v_ref[0, k0:lc, :]
        zero = jnp.zeros_like(q)
        e0 = _exp_scores(jnp.where(lane < DIFF_HALF, q, zero), k)
        e1 = _exp_scores(jnp.where(lane < DIFF_HALF, zero, q), k)
        o = (_dot(e0.astype(BF16), v) * (1.0 / jnp.sum(e0, axis=-1, keepdims=True))
             - _dot(e1.astype(BF16), v) * (lam / jnp.sum(e1, axis=-1, keepdims=True)))
        o_ref[0, r0:r0 + rows, :] = (_rms(o, g_ref[...]) * (1.0 - lam_init)).astype(o_ref.dtype)


def _attention(body, name, q, k, v, extras, scratch, n_lat, has_ctx, heads, wq, wv, wo):
    b, lc, _ = q.shape
    rows = lc if has_ctx else n_lat
    full = lambda n, w: pl.BlockSpec((1, n, w), lambda bi, h: (bi, 0, h))
    small = [pl.BlockSpec(a.shape, lambda bi, h: (0, 0)) for a in extras]
    return pl.pallas_call(
        body,
        grid=(b, heads),
        in_specs=[full(rows, wq), full(lc, wq), full(lc, wv)] + small,
        out_specs=full(rows, wo),
        out_shape=jax.ShapeDtypeStruct((b, rows, heads * wo), BF16),
        scratch_shapes=scratch,
        compiler_params=pltpu.CompilerParams(
            dimension_semantics=("parallel", "parallel"), vmem_limit_bytes=VMEM_LIMIT),
        name=name,
    )(q, k, v, *extras)


POOL_HALO = 16


def _pool_kernel(u_ref, w_ref, s_ref, o_ref, pad_ref, *, segments):
    for s0, n in segments:
        t = lax.broadcasted_iota(I32, (n, 1), 0)
        for g, w in enumerate(POOL_WINDOWS):
            cols = slice(g * POOL_GROUP_W, (g + 1) * POOL_GROUP_W)
            xg = u_ref[0, s0:s0 + n, cols].astype(F32)
            zeros = jnp.zeros((POOL_HALO, POOL_GROUP_W), F32)
            pad_ref[0:POOL_HALO, :] = zeros
            pad_ref[POOL_HALO + n:2 * POOL_HALO + n, :] = zeros
            pad_ref[POOL_HALO:POOL_HALO + n, :] = xg
            acc = pad_ref[POOL_HALO - w // 2:POOL_HALO - w // 2 + n, :]
            for j in range(1 - w // 2, w // 2):
                acc = acc + pad_ref[POOL_HALO + j:POOL_HALO + j + n, :]
            cnt = (jnp.minimum(t + w // 2, n) - jnp.maximum(t - w // 2, 0)).astype(F32)
            pooled = (acc / cnt - xg).astype(BF16)
            o_ref[0, s0:s0 + n, cols] = (_dot(pooled, w_ref[g]) * s_ref[:, cols]).astype(o_ref.dtype)


def _pool_mix(pool_in, w_pool, pool_scale, segments):
    b, lc, pw = pool_in.shape
    nmax = max(n for _, n in segments)
    lo = sum(n for _, n in segments)
    return pl.pallas_call(
        functools.partial(_pool_kernel, segments=segments),
        grid=(b,),
        in_specs=[
            pl.BlockSpec((1, lc, pw), lambda bi: (bi, 0, 0)),
            pl.BlockSpec(w_pool.shape, lambda bi: (0, 0, 0)),
            pl.BlockSpec(pool_scale.shape, lambda bi: (0, 0)),
        ],
        out_specs=pl.BlockSpec((1, lo, pw), lambda bi: (bi, 0, 0)),
        out_shape=jax.ShapeDtypeStruct((b, lo, pw), BF16),
        scratch_shapes=[pltpu.VMEM((nmax + 2 * POOL_HALO, POOL_GROUP_W), F32)],
        compiler_params=pltpu.CompilerParams(dimension_semantics=("parallel",), vmem_limit_bytes=VMEM_LIMIT),
        name="pool_mix",
    )(pool_in, w_pool, pool_scale)


MERGE_GROUP = 2


def _merge_kernel(om_ref, od_ref, op_ref, gate_ref, h_ref, mod_ref, wb_ref, wo_ref, gpost_ref, gffn_ref, wr_ref,
                  h2_ref, fx_ref, lg_ref, *, tiles_per_sample, n_lat_tiles, groups):
    d = h_ref.shape[-1]
    merged = (gate_ref[0, :, 0:d].astype(F32) * _dot(om_ref[0], wb_ref[0])
              + gate_ref[0, :, d:2 * d].astype(F32) * _dot(op_ref[0], wb_ref[1])
              + gate_ref[0, :, 2 * d:3 * d].astype(F32) * _dot(od_ref[0], wb_ref[2]))
    y = _dot(merged.astype(BF16), wo_ref[...])
    step = pl.program_id(0) * pl.num_programs(1) + pl.program_id(1)
    his, los = [], []
    for g in range(groups):
        rows = slice(g * TM, (g + 1) * TM)
        tile = step * groups + g
        sample = tile // tiles_per_sample
        m = jnp.where(tile - sample * tiles_per_sample < n_lat_tiles, mod_ref[sample, 1], mod_ref[sample, 0])
        h2 = h_ref[0, rows, :] + m[2:3] * _rms(y[rows], gpost_ref[...])
        h2_ref[0, rows, :] = h2
        fhi, flo = _split_bf16(_rms(h2, gffn_ref[...]) * (1.0 + m[4:5]) + m[3:4])
        fx_ref[0, rows, :] = fhi
        his.append(fhi)
        los.append(flo)
    fhi = jnp.concatenate(his, axis=0)
    flo = jnp.concatenate(los, axis=0)
    whi, wlo = _split_bf16(wr_ref[...])
    lg_ref[...] = _dot_nt(whi, fhi) + _dot_nt(whi, flo) + _dot_nt(wlo, fhi)


def _merge(o_mla, o_diff, o_pool, gates, h, mods, wb, w_out, g_post, g_ffn, w_rt, n_lat, nt):
    b, lc, d = h.shape
    ne = w_rt.shape[0]
    groups = MERGE_GROUP
    tg = groups * TM
    if nt % groups == 0:
        outer, steps, rows_out = b, nt // groups, nt * TM
        arrays = [o_mla, o_diff, o_pool, gates, h]
    else:
        assert nt * TM == lc and (b * nt) % groups == 0 and all(a.shape[1] == lc for a in (o_mla, o_diff, o_pool))
        outer, steps, rows_out = 1, b * nt // groups, b * lc
        arrays = [a.reshape(1, b * lc, a.shape[-1]) for a in (o_mla, o_diff, o_pool, gates, h)]
    tile = lambda w: pl.BlockSpec((1, tg, w), lambda o, i: (o, i, 0))
    h2, fx, logits_t = pl.pallas_call(
        functools.partial(_merge_kernel, tiles_per_sample=nt, n_lat_tiles=n_lat // TM, groups=groups),
        grid=(outer, steps),
        in_specs=[tile(a.shape[-1]) for a in arrays] + [
            _const_spec(a.shape) for a in (mods, wb, w_out, g_post, g_ffn, w_rt)],
        out_specs=[tile(d), tile(d), pl.BlockSpec((ne, tg), lambda o, i: (0, o * steps + i))],
        out_shape=[jax.ShapeDtypeStruct((outer, rows_out, d), F32), jax.ShapeDtypeStruct((outer, rows_out, d), BF16),
                   jax.ShapeDtypeStruct((ne, b * nt * TM), F32)],
        compiler_params=pltpu.CompilerParams(
            dimension_semantics=("parallel", "parallel"), vmem_limit_bytes=VMEM_LIMIT),
        name="merge",
    )(*arrays, mods, wb, w_out, g_post, g_ffn, w_rt)
    return h2.reshape(b, nt * TM, d), fx.reshape(b, nt * TM, d), logits_t


def _route_kernel(lg_ref, br_ref, posm_ref, wts_ref, meta_ref, tot_ref, carry_ref):
    ne, tm = lg_ref.shape
    gsz = ne // N_GROUPS
    neg = -jnp.inf

    @pl.when(pl.program_id(0) == 0)
    def _():
        carry_ref[...] = jnp.zeros_like(carry_ref)

    scores = jax.nn.sigmoid(lg_ref[...])
    sel = scores + br_ref[...]

    jidx = lax.broadcasted_iota(I32, (gsz, tm), 0).astype(F32)
    gscore = []
    for g in range(N_GROUPS):
        v = sel[g * gsz:(g + 1) * gsz]
        m1 = jnp.max(v, axis=0, keepdims=True)
        first = jnp.min(jnp.where(v == m1, jidx, float(gsz)), axis=0, keepdims=True)
        m2 = jnp.max(jnp.where(jidx == first, neg, v), axis=0, keepdims=True)
        gscore.append(m1 + m2)
    masked = []
    for g in range(N_GROUPS):
        rank = jnp.zeros((1, tm), I32)
        for o in range(N_GROUPS):
            if o == g:
                continue
            ahead = gscore[o] > gscore[g]
            if o < g:
                ahead = ahead | (gscore[o] == gscore[g])
            rank = rank + ahead.astype(I32)
        masked.append(jnp.where(rank < TOPK_GROUPS, sel[g * gsz:(g + 1) * gsz], neg))
    cur = jnp.concatenate(masked, axis=0)

    eidx = lax.broadcasted_iota(I32, (ne, tm), 0).astype(F32)
    chosen = jnp.zeros((ne, tm), jnp.bool_)
    for _ in range(TOP_K):
        mx = jnp.max(cur, axis=0, keepdims=True)
        first = jnp.min(jnp.where(cur == mx, eidx, float(ne)), axis=0, keepdims=True)
        hit = eidx == first
        chosen = chosen | hit
        cur = jnp.where(hit, neg, cur)

    wsel = jnp.where(chosen, scores, 0.0)
    wts_ref[...] = wsel / jnp.sum(wsel, axis=0, keepdims=True) * ROUTED_SCALE

    cmask = chosen.astype(BF16)
    ti = lax.broadcasted_iota(I32, (tm, tm), 0)
    tj = lax.broadcasted_iota(I32, (tm, tm), 1)
    pos = _dot(cmask, (ti < tj).astype(BF16))
    posm_ref[...] = jnp.where(chosen, pos, -1.0)
    ei = lax.broadcasted_iota(I32, (ne, ne), 0)
    ej = lax.broadcasted_iota(I32, (ne, ne), 1)
    cnt = jnp.sum(chosen.astype(F32), axis=1, keepdims=True)
    cpad = jnp.floor((cnt + (ALIGN - 1)) * (1.0 / ALIGN)) * ALIGN
    cpad_b = jnp.broadcast_to(cpad, (ne, LANES))
    loc_b = _dot((ej < ei).astype(BF16), cpad_b.astype(BF16))

    gpos = carry_ref[...]
    lane = lax.broadcasted_iota(I32, (ne, LANES), 1)
    meta_ref[0] = jnp.where(lane == 0, cpad_b, jnp.where(lane == 1, loc_b, gpos)).astype(I32)
    carry_ref[...] = gpos + cpad_b
    tot_ref[...] = (gpos + cpad_b).astype(I32)


def _route(logits_t, b_router):
    ne, ntok = logits_t.shape
    ntiles = ntok // TM
    return pl.pallas_call(
        _route_kernel,
        grid=(ntiles,),
        in_specs=[pl.BlockSpec((ne, TM), lambda t: (0, t)), pl.BlockSpec((ne, 1), lambda t: (0, 0))],
        out_specs=[
            pl.BlockSpec((ne, TM), lambda t: (0, t)),
            pl.BlockSpec((ne, TM), lambda t: (0, t)),
            pl.BlockSpec((1, ne, LANES), lambda t: (t, 0, 0)),
            pl.BlockSpec((ne, LANES), lambda t: (0, 0)),
        ],
        out_shape=[
            jax.ShapeDtypeStruct((ne, ntok), F32),
            jax.ShapeDtypeStruct((ne, ntok), F32),
            jax.ShapeDtypeStruct((ntiles, ne, LANES), I32),
            jax.ShapeDtypeStruct((ne, LANES), I32),
        ],
        scratch_shapes=[pltpu.VMEM((ne, LANES), F32)],
        compiler_params=pltpu.CompilerParams(dimension_semantics=("arbitrary",)),
        name="route",
    )(logits_t, b_router)


WAIT_ROWS = 16 * ALIGN
START_UNROLL = 4


def _copy(local, remote, l0, r0, rows, sem, to_remote):
    lrows = local.at[pl.ds(l0, rows)]
    rrows = remote.at[pl.ds(r0, rows)]
    return pltpu.make_async_copy(lrows, rrows, sem) if to_remote else pltpu.make_async_copy(rrows, lrows, sem)


def _start_chunks(rem_ref, n, local, remote, sem, to_remote):
    def start(c):
        _copy(local, remote, pl.multiple_of(c * ALIGN, ALIGN), pl.multiple_of(rem_ref[0, 0, c], ALIGN), ALIGN,
              sem, to_remote).start()

    def per_group(g, carry):
        for u in range(START_UNROLL):
            start(g * START_UNROLL + u)
        return carry

    groups = n // START_UNROLL
    lax.fori_loop(0, groups, per_group, 0)
    lax.fori_loop(groups * START_UNROLL, n, lambda c, carry: (start(c), carry)[1], 0)


def _wait_chunks(n, local, remote, sem, to_remote):
    per = WAIT_ROWS // ALIGN
    lax.fori_loop(0, n // per, lambda i, c: (_copy(local, remote, 0, 0, WAIT_ROWS, sem, to_remote).wait(), c)[1], 0)
    lax.fori_loop(0, lax.rem(n, per), lambda i, c: (_copy(local, remote, 0, 0, ALIGN, sem, to_remote).wait(), c)[1], 0)


def _dispatch_kernel(n16_ref, tail_ref, tailn_ref, nused_ref, fx_ref, posm_ref, lohi_ref, rem_ref, xs_ref,
                     buf_ref, zero_ref, sem):
    t = pl.program_id(0)
    slot = lax.rem(t, 2)
    x = fx_ref[0]
    buf = buf_ref.at[slot]
    posm = posm_ref[...]
    m2 = jnp.concatenate([posm, jnp.zeros((LANES - posm.shape[0], posm.shape[1]), F32)], axis=0).astype(BF16)
    lo = lohi_ref[0, 0:1, :]
    hi = lohi_ref[0, 1:2, :]
    def sort_chunk(rc):
        r = (lax.broadcasted_iota(I32, (RCH, 1), 0) + rc * RCH).astype(F32)
        member = (r >= lo) & (r < hi)
        base = jnp.sum(jnp.where(member, lo, 0.0), axis=1, keepdims=True)
        onehot = (_dot(member.astype(BF16), m2) == r - base).astype(BF16)
        buf[rc * RCH:(rc + 1) * RCH, :] = _dot(onehot, x).astype(buf_ref.dtype)

    last_rc = RL // RCH - 1
    for rc in range(last_rc):
        sort_chunk(rc)
    pl.when(n16_ref[t] * ALIGN > last_rc * RCH)(lambda: sort_chunk(last_rc))

    @pl.when(t > 0)
    def _():
        _wait_chunks(n16_ref[t - 1], buf_ref.at[1 - slot], xs_ref, sem.at[0], True)

    _start_chunks(rem_ref, n16_ref[t], buf, xs_ref, sem.at[0], True)

    def tail(act):
        def go(cp):
            if act == 0:
                cp.start()
            else:
                cp.wait()

        def per_expert(e, carry):
            s0 = tail_ref[e]

            def per_chunk(j, c):
                go(pltpu.make_async_copy(
                    zero_ref.at[pl.ds(0, ALIGN)],
                    xs_ref.at[pl.ds(pl.multiple_of(s0 + j * ALIGN, ALIGN), ALIGN)], sem.at[1]))
                return c

            lax.fori_loop(0, tailn_ref[e], per_chunk, 0)
            return carry

        lax.fori_loop(0, N_EXPERTS, per_expert, 0)

        def per_block(j, c):
            go(pltpu.make_async_copy(zero_ref, xs_ref.at[pl.ds(pl.multiple_of(j * EBLK, EBLK), EBLK)], sem.at[1]))
            return c

        lax.fori_loop(nused_ref[0], xs_ref.shape[0] // EBLK, per_block, 0)

    @pl.when(t == pl.num_programs(0) - 1)
    def _():
        zero_ref[...] = jnp.zeros_like(zero_ref)
        tail(0)
        _wait_chunks(n16_ref[t], buf, xs_ref, sem.at[0], True)
        tail(1)


def _dispatch(fx, posm, lohi, tabs, rmax):
    b, rows, d = fx.shape
    nt = rows // TM
    ntiles = b * nt
    rem = tabs["rem"]
    return pl.pallas_call(
        _dispatch_kernel,
        grid_spec=pltpu.PrefetchScalarGridSpec(
            num_scalar_prefetch=4,
            grid=(ntiles,),
            in_specs=[
                pl.BlockSpec((1, TM, d), lambda t, *_: (t // nt, t % nt, 0)),
                pl.BlockSpec((posm.shape[0], TM), lambda t, *_: (0, t)),
                pl.BlockSpec((1,) + lohi.shape[1:], lambda t, *_: (t, 0, 0)),
                pl.BlockSpec((1,) + rem.shape[1:], lambda t, *_: (t, 0, 0), memory_space=pltpu.SMEM),
            ],
            out_specs=pl.BlockSpec(memory_space=pl.ANY),
            scratch_shapes=[pltpu.VMEM((2, RL, d), BF16), pltpu.VMEM((EBLK, d), BF16),
                            pltpu.SemaphoreType.DMA((2,))],
        ),
        out_shape=jax.ShapeDtypeStruct((rmax, d), BF16),
        compiler_params=pltpu.CompilerParams(dimension_semantics=("arbitrary",), vmem_limit_bytes=VMEM_LIMIT),
        name="dispatch",
    )(tabs["n16"], tabs["tail_start"], tabs["tail_n16"], tabs["n_used"], fx, posm, lohi, rem)


def _expert_kernel(blk_ref, nused_ref, x_ref, wg_ref, wu_ref, wd_ref, y_ref, wgu_bf, wd_bf):
    i = pl.program_id(0)

    @pl.when((i == 0) | (blk_ref[i] != blk_ref[jnp.maximum(i - 1, 0)]))
    def _():
        wgu_bf[:, :EXPERT_FF] = wg_ref[0].astype(BF16)
        wgu_bf[:, EXPERT_FF:] = wu_ref[0].astype(BF16)
        wd_bf[...] = wd_ref[0].astype(BF16)

    @pl.when(i < nused_ref[0])
    def _():
        gu = _dot(x_ref[...], wgu_bf[...])
        hmid = (_silu(gu[:, :EXPERT_FF]) * gu[:, EXPERT_FF:]).astype(BF16)
        y_ref[...] = _dot(hmid, wd_bf[...]).astype(y_ref.dtype)


def _experts(xs, blk_e, n_used, w_g, w_u, w_d, layer):
    rmax, d = xs.shape
    nblk = rmax // EBLK
    row = lambda i, blk, nu: (jnp.minimum(i, nu[0] - 1), 0)
    wsel = lambda i, blk, nu: (layer, blk[i], 0, 0)
    return pl.pallas_call(
        _expert_kernel,
        grid_spec=pltpu.PrefetchScalarGridSpec(
            num_scalar_prefetch=2,
            grid=(nblk,),
            in_specs=[
                pl.BlockSpec((EBLK, d), row),
                pl.BlockSpec((None, 1, d, EXPERT_FF), wsel),
                pl.BlockSpec((None, 1, d, EXPERT_FF), wsel),
                pl.BlockSpec((None, 1, EXPERT_FF, d), wsel),
            ],
            out_specs=pl.BlockSpec((EBLK, d), row),
            scratch_shapes=[pltpu.VMEM((d, 2 * EXPERT_FF), BF16), pltpu.VMEM((EXPERT_FF, d), BF16)],
        ),
        out_shape=jax.ShapeDtypeStruct((rmax, d), BF16),
        input_output_aliases={2: 0},
        compiler_params=pltpu.CompilerParams(dimension_semantics=("arbitrary",), vmem_limit_bytes=VMEM_LIMIT),
        name="experts",
    )(blk_e, n_used, xs, w_g, w_u, w_d)


def _combine_kernel(n16_ref, y_ref, posmt_ref, wt_ref, meta_ref, fx_ref, h_ref, mod_ref, wsgu_ref,
                    wsd_ref, gpost_ref, rem_ref, remn_ref, o_ref, buf_ref, sem):
    t = pl.program_id(0)
    slot = lax.rem(t, 2)
    buf = buf_ref.at[slot]

    @pl.when(t == 0)
    def _():
        buf_ref[...] = jnp.zeros_like(buf_ref)
        _start_chunks(rem_ref, n16_ref[t], buf, y_ref, sem.at[slot], False)

    @pl.when(t + 1 < pl.num_programs(0))
    def _():
        _start_chunks(remn_ref, n16_ref[t + 1], buf_ref.at[1 - slot], y_ref, sem.at[1 - slot], False)

    x = fx_ref[0]
    gu = _dot(x, wsgu_ref[...])
    ff = wsd_ref.shape[0]
    o_ref[0] = _dot((_silu(gu[:, :ff]) * gu[:, ff:]).astype(BF16), wsd_ref[...])

    _wait_chunks(n16_ref[t], buf, y_ref, sem.at[slot], False)

    posmt = posmt_ref[...].astype(BF16)
    wt = wt_ref[...].astype(BF16)
    lo = meta_ref[0, :, 1:2].astype(F32)
    hi = lo + meta_ref[0, :, 0:1].astype(F32)
    def unsort_chunk(rc):
        r = (lax.broadcasted_iota(I32, (1, RCH), 1) + rc * RCH).astype(F32)
        member = (r >= lo) & (r < hi)
        base = jnp.sum(jnp.where(member, lo, 0.0), axis=0, keepdims=True)
        mb = member.astype(BF16)
        pw = jnp.where(_dot(posmt, mb) == r - base, _dot(wt, mb), 0.0).astype(BF16)
        o_ref[0] += _dot(pw, buf[rc * RCH:(rc + 1) * RCH, :])

    last_rc = RL // RCH - 1
    for rc in range(last_rc):
        unsort_chunk(rc)
    pl.when(n16_ref[t] * ALIGN > last_rc * RCH)(lambda: unsort_chunk(last_rc))

    m = mod_ref[0, 0]
    o_ref[0] = h_ref[0] + m[5:6] * _rms(o_ref[0], gpost_ref[...])


def _combine(y, posm_t, wts_t, meta, fx, h2, mods, tabs, w_sgu, w_sd, g_post, n_lat):
    b, rows, d = fx.shape
    nt = rows // TM
    ntiles = b * nt
    nlt = n_lat // TM
    ne = posm_t.shape[1]
    rem = tabs["rem"]
    tile = lambda t, *_: (t // nt, t % nt, 0)
    return pl.pallas_call(
        _combine_kernel,
        grid_spec=pltpu.PrefetchScalarGridSpec(
            num_scalar_prefetch=1,
            grid=(ntiles,),
            in_specs=[
                pl.BlockSpec(memory_space=pl.ANY),
                pl.BlockSpec((TM, ne), lambda t, *_: (t, 0)),
                pl.BlockSpec((TM, ne), lambda t, *_: (t, 0)),
                pl.BlockSpec((1,) + meta.shape[1:], lambda t, *_: (t, 0, 0)),
                pl.BlockSpec((1, TM, d), tile),
                pl.BlockSpec((1, TM, d), tile),
                pl.BlockSpec((1, 1, 6, d), lambda t, *_: (t // nt, jnp.where(t % nt < nlt, 1, 0), 0, 0)),
                pl.BlockSpec(w_sgu.shape, lambda t, *_: (0, 0), pipeline_mode=pl.Buffered(1)),
                pl.BlockSpec(w_sd.shape, lambda t, *_: (0, 0), pipeline_mode=pl.Buffered(1)),
                pl.BlockSpec(g_post.shape, lambda t, *_: (0, 0), pipeline_mode=pl.Buffered(1)),
                pl.BlockSpec((1,) + rem.shape[1:], lambda t, *_: (t, 0, 0), memory_space=pltpu.SMEM),
                pl.BlockSpec((1,) + rem.shape[1:], lambda t, *_: (jnp.minimum(t + 1, ntiles - 1), 0, 0),
                             memory_space=pltpu.SMEM),
            ],
            out_specs=pl.BlockSpec((1, TM, d), tile),
            scratch_shapes=[pltpu.VMEM((2, RL, d), BF16), pltpu.SemaphoreType.DMA((2,))],
        ),
        out_shape=jax.ShapeDtypeStruct((b, rows, d), F32),
        compiler_params=pltpu.CompilerParams(dimension_semantics=("arbitrary",), vmem_limit_bytes=VMEM_LIMIT),
        name="combine",
    )(tabs["n16"], y, posm_t, wts_t, meta, fx, h2, mods, w_sgu, w_sd, g_post, rem, rem)


def _rope_tables(n_lat, n_ctx, rot_dim, lane_slots):
    rows = n_lat // GRID_W
    pos_row = jnp.repeat(jnp.arange(rows, dtype=F32), GRID_W)
    pos_col = jnp.tile(jnp.arange(GRID_W, dtype=F32), rows)
    d_axis = rot_dim // 2
    inv_freq = ROPE_BASE ** (-jnp.arange(0, d_axis, 2, dtype=F32) / d_axis)
    ar = pos_row[:, None] * inv_freq
    ac = pos_col[:, None] * inv_freq
    ang = jnp.concatenate([ar, ar, ac, ac], axis=1)
    qtr = rot_dim // 4
    first = (jnp.arange(rot_dim) // qtr) % 2 == 0
    cos, sin = jnp.cos(ang), jnp.sin(ang)
    s1 = jnp.where(first, -sin, 0.0)
    s2 = jnp.where(first, 0.0, sin)
    lc = n_lat + n_ctx
    c_t = jnp.ones((lc, LANES), F32)
    s1_t = jnp.zeros((lc, LANES), F32)
    s2_t = jnp.zeros((lc, LANES), F32)
    for lo in lane_slots:
        c_t = c_t.at[:n_lat, lo:lo + rot_dim].set(cos)
        s1_t = s1_t.at[:n_lat, lo:lo + rot_dim].set(s1)
        s2_t = s2_t.at[:n_lat, lo:lo + rot_dim].set(s2)
    return jnp.stack([c_t, s1_t, s2_t])


def _pack_layer_weights(w_in, w_uq, w_ukv):
    d = w_in.shape[0]
    kv0 = 0
    kr0 = kv0 + MLA_KV_RANK
    kd0 = kr0 + MLA_ROPE
    vd0 = kd0 + 512
    q0 = vd0 + 512
    qd0 = q0 + MLA_Q_RANK
    p0 = qd0 + 512
    g0 = p0 + 512
    kr = jnp.zeros((d, LANES), w_in.dtype).at[:, MLA_NOPE:MLA_NOPE + MLA_ROPE].set(w_in[:, kr0:kd0])
    w1 = jnp.concatenate([w_in[:, kv0:kr0], kr, w_in[:, kd0:vd0], w_in[:, vd0:q0], w_in[:, q0:qd0],
                          w_in[:, qd0:p0], w_in[:, p0:g0], w_in[:, g0:]], axis=1).astype(BF16)
    ukv = w_ukv.reshape(MLA_KV_RANK, MLA_HEADS, MLA_NOPE + MLA_V)
    uk = jnp.pad(ukv[:, :, :MLA_NOPE], ((0, 0), (0, 0), (0, LANES - MLA_NOPE))).reshape(MLA_KV_RANK, -1)
    uv = ukv[:, :, MLA_NOPE:].reshape(MLA_KV_RANK, -1)
    w_ukv_p = jnp.concatenate([uk, uv], axis=1).astype(BF16)
    uq = w_uq.reshape(MLA_Q_RANK, MLA_HEADS, MLA_NOPE + MLA_ROPE)
    w_uq_p = jnp.pad(uq, ((0, 0), (0, 0), (0, LANES - MLA_NOPE - MLA_ROPE))).reshape(MLA_Q_RANK, -1).astype(BF16)
    return w1, w_ukv_p, w_uq_p


def _routing_tables(meta, totals, rmax):
    cpad, loc, gpos = meta[:, :, 0], meta[:, :, 1], meta[:, :, 2]
    tot = totals[:, 0]
    reg = (tot + EBLK - 1) // EBLK * EBLK
    reg_end = jnp.cumsum(reg)
    reg_start = reg_end - reg
    n_used = jnp.maximum(reg_end[-1:] // EBLK, 1).astype(I32)
    blk_start = jnp.arange(rmax // EBLK, dtype=I32) * EBLK
    blk_e = jnp.minimum(jnp.sum(reg_end[None, :] <= blk_start[:, None], axis=1), N_EXPERTS - 1).astype(I32)
    ntiles, ne = loc.shape
    lohi = jnp.zeros((ntiles, 8, LANES), F32).at[:, 0, :].set(float(RL)).at[:, 0, :ne].set(loc.astype(F32))
    lohi = lohi.at[:, 1, :ne].set((loc + cpad).astype(F32))
    chunk = jnp.arange(RL // ALIGN, dtype=I32)[None, :, None]
    seg_lo = (loc // ALIGN)[:, None, :]
    seg_end = (loc + cpad) // ALIGN
    inside = (seg_lo <= chunk) & (chunk < seg_end[:, None, :])
    shift = jnp.sum(jnp.where(inside, (reg_start[None, :] + gpos - loc)[:, None, :], 0), axis=-1)
    rem = (shift + chunk[:, :, 0] * ALIGN).astype(I32)[:, None, :]
    return dict(
        n16=seg_end[:, -1].astype(I32), rem=rem,
        tail_start=(reg_start + tot).astype(I32), tail_n16=((reg - tot) // ALIGN).astype(I32),
        n_used=n_used), blk_e, lohi


def kernel(x, c, ctx, c_ctx, w_mod, b_mod, g_pre_mix, g_post_mix, g_pre_ffn, g_post_ffn, w_in, g_cq, w_uq, g_ckv, w_ukv, w_pool, pool_scale, lam_q1, lam_k1, lam_q2, lam_k2, g_subln, w_b_mla, w_b_pool, w_b_diff, w_out, w_router, b_router, w_e_gate, w_e_up, w_e_down, w_s_gate, w_s_up, w_s_down):
    b, n_lat, d = x.shape
    n_ctx = ctx.shape[1]
    depth = w_mod.shape[0]
    assert n_ctx == TM and n_lat % QROWS == 0 and n_lat % GRID_W == 0
    lc = n_lat + n_ctx
    row2 = lambda v: v.reshape(1, -1)

    rows = (b + 1 + 7) // 8 * 8
    cvec = jnp.zeros((rows, d), F32).at[:b].set(c).at[b].set(c_ctx)
    mod_all = _modulation(cvec, w_mod, b_mod).reshape(depth, rows, 6, d)

    tab_m = _rope_tables(n_lat, n_ctx, MLA_ROPE, (MLA_NOPE,))
    tab_d = _rope_tables(n_lat, n_ctx, DIFF_HALF, (0, DIFF_HALF))

    h = jnp.concatenate([x, ctx], axis=1)
    for l in range(depth):
        last = l == depth - 1
        lam_init = 0.8 - 0.6 * math.exp(-0.3 * l)
        mods = jnp.stack([jnp.broadcast_to(mod_all[l, b], (b, 6, d)), mod_all[l, :b]], axis=1)
        w1, w_ukv_p, w_uq_p = _pack_layer_weights(w_in[l], w_uq[l], w_ukv[l])

        qm, km, vm, qd, kd, vd, pool_in, gates = _project(
            h, mods, row2(g_pre_mix[l]), w1, row2(g_ckv[l]), w_ukv_p, row2(g_cq[l]), w_uq_p, tab_m, tab_d, n_lat)
        o_mla = _attention(
            functools.partial(_mla_kernel, n_lat=n_lat, has_ctx=not last), "mla_attention", qm, km, vm, (),
            [pltpu.VMEM((2, lc, 2 * MLA_V), BF16)], n_lat, not last, MLA_HEADS // 2, 2 * LANES, 2 * MLA_V, 2 * MLA_V)
        lam_p = jnp.stack([lam_q1[l], lam_k1[l], lam_q2[l], lam_k2[l]])
        o_diff = _attention(
            functools.partial(_diff_kernel, n_lat=n_lat, has_ctx=not last, lam_init=lam_init), "diff_attention",
            qd, kd, vd, (lam_p, row2(g_subln[l])), [], n_lat, not last, DIFF_HEADS, LANES, DIFF_V, DIFF_V)
        segments = ((0, n_lat),) if last else ((0, n_lat), (n_lat, n_ctx))
        o_pool = _pool_mix(pool_in, w_pool[l].astype(BF16), row2(pool_scale[l]), segments)

        nt = (n_lat if last else lc) // TM
        wb = jnp.stack([w_b_mla[l], w_b_pool[l], w_b_diff[l]]).astype(BF16)
        h2, fx, logits_t = _merge(o_mla, o_diff, o_pool, gates, h, mods, wb, w_out[l].astype(BF16),
                                  row2(g_post_mix[l]), row2(g_pre_ffn[l]), w_router[l].T, n_lat, nt)

        posm, wts, meta, totals = _route(logits_t, b_router[l].reshape(-1, 1))
        rmax = meta.shape[0] * RL + N_EXPERTS * EBLK
        tabs, blk_e, lohi = _routing_tables(meta, totals, rmax)
        xs = _dispatch(fx, posm, lohi, tabs, rmax)
        ys = _experts(xs, blk_e, tabs["n_used"], w_e_gate, w_e_up, w_e_down, l)
        w_sgu = jnp.concatenate([w_s_gate[l], w_s_up[l]], axis=-1).astype(BF16)
        h = _combine(ys, posm.T, wts.T, meta, fx, h2, mods, tabs, w_sgu, w_s_down[l].astype(BF16),
                     row2(g_post_ffn[l]), n_lat)
    return h
```

```python
import functools
import math

import jax
import jax.numpy as jnp
from jax import lax
from jax.experimental import pallas as pl
from jax.experimental.pallas import tpu as pltpu

F32 = jnp.float32
BF16 = jnp.bfloat16
I32 = jnp.int32

EPS = 1e-6
ROPE_BASE = 10000.0
GRID_W = 64

MLA_HEADS = 8
MLA_NOPE = 64
MLA_ROPE = 32
MLA_V = 64
MLA_Q_RANK = 384
MLA_KV_RANK = 256
POOL_WINDOWS = (2, 4, 8, 16)
POOL_GROUP_W = 128
DIFF_HEADS = 4
DIFF_HALF = 64
DIFF_V = 128
N_EXPERTS = 64
TOP_K = 8
N_GROUPS = 8
TOPK_GROUPS = 4
EXPERT_FF = 256
ROUTED_SCALE = 2.5

LANES = 128
TM = 256
QROWS = 256
ALIGN = 16
EBLK = 1024
RL = TM * TOP_K + N_EXPERTS * ALIGN
RCH = 512
VMEM_LIMIT = 56 * 1024 * 1024
LOG2E = 1.4426950408889634

C_CKV = 0
C_KR = C_CKV + MLA_KV_RANK
C_KD = C_KR + LANES
C_VD = C_KD + 512
C_CQ = C_VD + 512
C_QD = C_CQ + MLA_Q_RANK
C_POOL = C_QD + 512
C_GATE = C_POOL + 512


def _rms(x, g):
    ms = jnp.mean(x * x, axis=-1, keepdims=True)
    return x * lax.rsqrt(ms + EPS) * g


def _dot(a, b):
    return jnp.dot(a, b, preferred_element_type=F32)


def _dot_nt(a, b):
    return lax.dot_general(a, b, (((1,), (1,)), ((), ())), preferred_element_type=F32)


def _split_bf16(x):
    hi = x.astype(BF16)
    lo = (x - hi.astype(F32)).astype(BF16)
    return hi, lo


def _silu(x):
    return x * jax.nn.sigmoid(x)


def _rope(x, tab, quarter):
    w = x.shape[-1]
    rep = w // LANES
    c, s1, s2 = (jnp.tile(t, (1, rep)) if rep > 1 else t for t in tab)
    return x * c + pltpu.roll(x, w - quarter, 1) * s1 + pltpu.roll(x, quarter, 1) * s2


def _const_spec(shape):
    nd = len(shape)
    return pl.BlockSpec(shape, lambda *_: (0,) * nd, pipeline_mode=pl.Buffered(1))


def _mod_kernel(a_ref, w_ref, b_ref, o_ref):
    a = _silu(a_ref[...])
    ahi, alo = _split_bf16(a)
    whi, wlo = _split_bf16(w_ref[...])
    o_ref[...] = _dot(ahi, whi) + _dot(alo, whi) + _dot(ahi, wlo) + b_ref[...]


def _modulation(cvec, w_mod, b_mod):
    nl, d, d6 = w_mod.shape
    rows = cvec.shape[0]
    nc = d6 // d
    return pl.pallas_call(
        _mod_kernel,
        grid=(nl, nc),
        in_specs=[
            pl.BlockSpec((rows, d), lambda l, j: (0, 0)),
            pl.BlockSpec((None, d, d), lambda l, j: (l, 0, j)),
            pl.BlockSpec((None, 1, d), lambda l, j: (l, 0, j)),
        ],
        out_specs=pl.BlockSpec((None, rows, d), lambda l, j: (l, 0, j)),
        out_shape=jax.ShapeDtypeStruct((nl, rows, d6), F32),
        name="modulation",
    )(cvec, w_mod, b_mod.reshape(nl, 1, d6))


PROJ_GROUP = 2


def _proj_kernel(h_ref, mod_ref, gpre_ref, w1_ref, gckv_ref, wukv_ref, gcq_ref, wuq_ref, tm_ref, td_ref,
                 qm_ref, km_ref, vm_ref, qd_ref, kd_ref, vd_ref, pool_ref, gate_ref, *, tiles_per_sample,
                 n_lat_tiles, groups):
    d = h_ref.shape[-1]
    hx, tabm, tabd = [], [], []
    for g in range(groups):
        tile = pl.program_id(0) * groups + g
        sample = tile // tiles_per_sample
        pos_tile = tile - sample * tiles_per_sample
        m = jnp.where(pos_tile < n_lat_tiles, mod_ref[sample, 1], mod_ref[sample, 0])
        pos = pl.ds(pl.multiple_of(pos_tile * TM, TM), TM)
        tabm.append([tm_ref[k, pos, :] for k in range(3)])
        tabd.append([td_ref[k, pos, :] for k in range(3)])
        hx.append((_rms(h_ref[g * TM:(g + 1) * TM, :], gpre_ref[...]) * (1.0 + m[1:2]) + m[0:1]).astype(BF16))
    hx = jnp.concatenate(hx, axis=0)
    tabm = tuple(jnp.concatenate([tg[k] for tg in tabm], axis=0) for k in range(3))
    tabd = tuple(jnp.concatenate([tg[k] for tg in tabd], axis=0) for k in range(3))

    ckv = _dot(hx, w1_ref[:, C_CKV:C_KR])
    kv = _dot(_rms(ckv, gckv_ref[...]).astype(BF16), wukv_ref[...])
    kr = _rope(_dot(hx, w1_ref[:, C_KR:C_KD]), tabm, MLA_ROPE // 4)
    nk = MLA_HEADS * LANES
    km_ref[...] = (kv[:, :nk] + jnp.tile(kr, (1, MLA_HEADS))).astype(BF16)
    vm_ref[...] = kv[:, nk:].astype(BF16)

    kd_ref[...] = _rope(_dot(hx, w1_ref[:, C_KD:C_VD]), tabd, DIFF_HALF // 4).astype(BF16)
    vd_ref[...] = _dot(hx, w1_ref[:, C_VD:C_CQ]).astype(BF16)

    cq = _dot(hx, w1_ref[:, C_CQ:C_QD])
    q = _dot(_rms(cq, gcq_ref[...]).astype(BF16), wuq_ref[...])
    qm_scale = LOG2E / math.sqrt(MLA_NOPE + MLA_ROPE)
    qm_ref[...] = (_rope(q, tabm, MLA_ROPE // 4) * qm_scale).astype(BF16)
    qd_scale = LOG2E / math.sqrt(DIFF_HALF)
    qd_ref[...] = (_rope(_dot(hx, w1_ref[:, C_QD:C_POOL]), tabd, DIFF_HALF // 4) * qd_scale).astype(BF16)
    pool_ref[...] = _dot(hx, w1_ref[:, C_POOL:C_GATE]).astype(BF16)
    for j in range(3):
        lo = C_GATE + j * d
        gate_ref[:, j * d:(j + 1) * d] = jax.nn.sigmoid(_dot(hx, w1_ref[:, lo:lo + d])).astype(BF16)


def _project(h, mods, g_pre, w1, g_ckv, w_ukv, g_cq, w_uq, tab_m, tab_d, n_lat):
    b, lc, d = h.shape
    nt = lc // TM
    groups = PROJ_GROUP if (b * nt) % PROJ_GROUP == 0 else 1
    tile = lambda w: pl.BlockSpec((groups * TM, w), lambda i: (i, 0))
    widths = (MLA_HEADS * LANES, MLA_HEADS * LANES, MLA_HEADS * MLA_V, 512, 512, 512, 512, 3 * d)
    outs = pl.pallas_call(
        functools.partial(_proj_kernel, tiles_per_sample=nt, n_lat_tiles=n_lat // TM, groups=groups),
        grid=(b * nt // groups,),
        in_specs=[tile(d)] + [_const_spec(a.shape) for a in (mods, g_pre, w1, g_ckv, w_ukv, g_cq, w_uq, tab_m, tab_d)],
        out_specs=[tile(w) for w in widths],
        out_shape=[jax.ShapeDtypeStruct((b * lc, w), BF16) for w in widths],
        compiler_params=pltpu.CompilerParams(dimension_semantics=("parallel",), vmem_limit_bytes=VMEM_LIMIT),
        name="project",
    )(h.reshape(b * lc, d), mods, g_pre, w1, g_ckv, w_ukv, g_cq, w_uq, tab_m, tab_d)
    return [o.reshape(b, lc, -1) for o in outs]


def _exp_scores(q, k):
    s = _dot_nt(q, k)
    return jnp.exp2(s - jnp.max(s, axis=-1, keepdims=True))


def _query_groups(n_lat, lc, has_ctx):
    groups = [(r0, QROWS, 0) for r0 in range(0, n_lat, QROWS)]
    return groups + ([(n_lat, lc - n_lat, n_lat)] if has_ctx else [])


def _mla_kernel(q_ref, k_ref, v_ref, o_ref, vext_ref, *, n_lat, has_ctx):
    lc = k_ref.shape[1]
    lane = lax.broadcasted_iota(I32, (1, LANES), 1)
    v = v_ref[0]
    one = jnp.ones_like(v)
    vext_ref[0] = jnp.where(lane < MLA_V, v, one)
    vext_ref[1] = jnp.where(lane < MLA_V, one, v)
    for r0, rows, k0 in _query_groups(n_lat, lc, has_ctx):
        pv = []
        for j in range(2):
            e = _exp_scores(q_ref[0, r0:r0 + rows, j * LANES:(j + 1) * LANES],
                            k_ref[0, k0:lc, j * LANES:(j + 1) * LANES])
            pv.append(_dot(e.astype(BF16), vext_ref[j, k0:lc, :]))
        o0 = pv[0] * (1.0 / pv[0][:, MLA_V:MLA_V + 1])
        o1 = pv[1] * (1.0 / pv[1][:, 0:1])
        o_ref[0, r0:r0 + rows, :] = jnp.where(lane < MLA_V, o0, o1).astype(o_ref.dtype)


def _diff_kernel(q_ref, k_ref, v_ref, lam_ref, g_ref, o_ref, *, n_lat, has_ctx, lam_init):
    lc = k_ref.shape[1]
    lane = lax.broadcasted_iota(I32, (1, LANES), 1)
    lp = lam_ref[...]
    lam = (jnp.exp(jnp.sum(lp[0:1] * lp[1:2], axis=-1, keepdims=True))
           - jnp.exp(jnp.sum(lp[2:3] * lp[3:4], axis=-1, keepdims=True)) + lam_init)
    for r0, rows, k0 in _query_groups(n_lat, lc, has_ctx):
        q = q_ref[0, r0:r0 + rows, :]
        k = k_ref[0, k0:lc, :]
        v = v_ref[0, k0:lc, :]
        zero = jnp.zeros_like(q)
        e0 = _exp_scores(jnp.where(lane < DIFF_HALF, q, zero), k)
        e1 = _exp_scores(jnp.where(lane < DIFF_HALF, zero, q), k)
        o = (_dot(e0.astype(BF16), v) * (1.0 / jnp.sum(e0, axis=-1, keepdims=True))
             - _dot(e1.astype(BF16), v) * (lam / jnp.sum(e1, axis=-1, keepdims=True)))
        o_ref[0, r0:r0 + rows, :] = (_rms(o, g_ref[...]) * (1.0 - lam_init)).astype(o_ref.dtype)


def _attention(body, name, q, k, v, extras, scratch, n_lat, has_ctx, heads, wq, wv, wo):
    b, lc, _ = q.shape
    rows = lc if has_ctx else n_lat
    full = lambda n, w: pl.BlockSpec((1, n, w), lambda bi, h: (bi, 0, h))
    small = [pl.BlockSpec(a.shape, lambda bi, h: (0, 0)) for a in extras]
    return pl.pallas_call(
        body,
        grid=(b, heads),
        in_specs=[full(rows, wq), full(lc, wq), full(lc, wv)] + small,
        out_specs=full(rows, wo),
        out_shape=jax.ShapeDtypeStruct((b, rows, heads * wo), BF16),
        scratch_shapes=scratch,
        compiler_params=pltpu.CompilerParams(
            dimension_semantics=("parallel", "parallel"), vmem_limit_bytes=VMEM_LIMIT),
        name=name,
    )(q, k, v, *extras)


POOL_HALO = 16


def _pool_kernel(u_ref, w_ref, s_ref, o_ref, pad_ref, *, segments):
    for s0, n in segments:
        t = lax.broadcasted_iota(I32, (n, 1), 0)
        for g, w in enumerate(POOL_WINDOWS):
            cols = slice(g * POOL_GROUP_W, (g + 1) * POOL_GROUP_W)
            xg = u_ref[0, s0:s0 + n, cols].astype(F32)
            zeros = jnp.zeros((POOL_HALO, POOL_GROUP_W), F32)
            pad_ref[0:POOL_HALO, :] = zeros
            pad_ref[POOL_HALO + n:2 * POOL_HALO + n, :] = zeros
            pad_ref[POOL_HALO:POOL_HALO + n, :] = xg
            acc = pad_ref[POOL_HALO - w // 2:POOL_HALO - w // 2 + n, :]
            for j in range(1 - w // 2, w // 2):
                acc = acc + pad_ref[POOL_HALO + j:POOL_HALO + j + n, :]
            cnt = (jnp.minimum(t + w // 2, n) - jnp.maximum(t - w // 2, 0)).astype(F32)
            pooled = (acc / cnt - xg).astype(BF16)
            o_ref[0, s0:s0 + n, cols] = (_dot(pooled, w_ref[g]) * s_ref[:, cols]).astype(o_ref.dtype)


def _pool_mix(pool_in, w_pool, pool_scale, segments):
    b, lc, pw = pool_in.shape
    nmax = max(n for _, n in segments)
    lo = sum(n for _, n in segments)
    return pl.pallas_call(
        functools.partial(_pool_kernel, segments=segments),
        grid=(b,),
        in_specs=[
            pl.BlockSpec((1, lc, pw), lambda bi: (bi, 0, 0)),
            pl.BlockSpec(w_pool.shape, lambda bi: (0, 0, 0)),
            pl.BlockSpec(pool_scale.shape, lambda bi: (0, 0)),
        ],
        out_specs=pl.BlockSpec((1, lo, pw), lambda bi: (bi, 0, 0)),
        out_shape=jax.ShapeDtypeStruct((b, lo, pw), BF16),
        scratch_shapes=[pltpu.VMEM((nmax + 2 * POOL_HALO, POOL_GROUP_W), F32)],
        compiler_params=pltpu.CompilerParams(dimension_semantics=("parallel",), vmem_limit_bytes=VMEM_LIMIT),
        name="pool_mix",
    )(pool_in, w_pool, pool_scale)


MERGE_GROUP = 2


def _merge_kernel(om_ref, od_ref, op_ref, gate_ref, h_ref, mod_ref, wb_ref, wo_ref, gpost_ref, gffn_ref, wr_ref,
                  br_ref, h2_ref, fx_ref, wts_ref, posm_ref, meta_ref, tot_ref, carry_ref, *, tiles_per_sample,
                  n_lat_tiles, groups):
    d = h_ref.shape[-1]

    @pl.when((pl.program_id(0) == 0) & (pl.program_id(1) == 0))
    def _():
        carry_ref[...] = jnp.zeros_like(carry_ref)

    merged = (gate_ref[0, :, 0:d].astype(F32) * _dot(om_ref[0], wb_ref[0])
              + gate_ref[0, :, d:2 * d].astype(F32) * _dot(op_ref[0], wb_ref[1])
              + gate_ref[0, :, 2 * d:3 * d].astype(F32) * _dot(od_ref[0], wb_ref[2]))
    y = _dot(merged.astype(BF16), wo_ref[...])
    step = pl.program_id(0) * pl.num_programs(1) + pl.program_id(1)
    his, los = [], []
    for g in range(groups):
        rows = slice(g * TM, (g + 1) * TM)
        tile = step * groups + g
        sample = tile // tiles_per_sample
        m = jnp.where(tile - sample * tiles_per_sample < n_lat_tiles, mod_ref[sample, 1], mod_ref[sample, 0])
        h2 = h_ref[0, rows, :] + m[2:3] * _rms(y[rows], gpost_ref[...])
        h2_ref[0, rows, :] = h2
        fhi, flo = _split_bf16(_rms(h2, gffn_ref[...]) * (1.0 + m[4:5]) + m[3:4])
        fx_ref[0, rows, :] = fhi
        his.append(fhi)
        los.append(flo)
    fhi = jnp.concatenate(his, axis=0)
    flo = jnp.concatenate(los, axis=0)
    whi, wlo = _split_bf16(wr_ref[...])
    logits = _dot_nt(whi, fhi) + _dot_nt(whi, flo) + _dot_nt(wlo, fhi)

    lane = lax.broadcasted_iota(I32, carry_ref.shape, 1)
    for g in range(groups):
        cols = slice(g * TM, (g + 1) * TM)
        wts, posm, cpad_b, loc_b = _route_tile(logits[:, cols], br_ref[...])
        wts_ref[:, cols] = wts
        posm_ref[:, cols] = posm
        gpos = carry_ref[...]
        meta_ref[g] = jnp.where(lane == 0, cpad_b, jnp.where(lane == 1, loc_b, gpos)).astype(I32)
        carry_ref[...] = gpos + cpad_b
    tot_ref[...] = carry_ref[...].astype(I32)


def _merge(o_mla, o_diff, o_pool, gates, h, mods, wb, w_out, g_post, g_ffn, w_rt, b_router, n_lat, nt):
    b, lc, d = h.shape
    ne = w_rt.shape[0]
    groups = MERGE_GROUP
    tg = groups * TM
    if nt % groups == 0:
        outer, steps, rows_out = b, nt // groups, nt * TM
        arrays = [o_mla, o_diff, o_pool, gates, h]
    else:
        assert nt * TM == lc and (b * nt) % groups == 0 and all(a.shape[1] == lc for a in (o_mla, o_diff, o_pool))
        outer, steps, rows_out = 1, b * nt // groups, b * lc
        arrays = [a.reshape(1, b * lc, a.shape[-1]) for a in (o_mla, o_diff, o_pool, gates, h)]
    tile = lambda w: pl.BlockSpec((1, tg, w), lambda o, i: (o, i, 0))
    tokens = pl.BlockSpec((ne, tg), lambda o, i: (0, o * steps + i))
    h2, fx, wts, posm, meta, totals = pl.pallas_call(
        functools.partial(_merge_kernel, tiles_per_sample=nt, n_lat_tiles=n_lat // TM, groups=groups),
        grid=(outer, steps),
        in_specs=[tile(a.shape[-1]) for a in arrays] + [
            _const_spec(a.shape) for a in (mods, wb, w_out, g_post, g_ffn, w_rt, b_router)],
        out_specs=[tile(d), tile(d), tokens, tokens,
                   pl.BlockSpec((groups, ne, LANES), lambda o, i: (o * steps + i, 0, 0)),
                   pl.BlockSpec((ne, LANES), lambda o, i: (0, 0))],
        out_shape=[jax.ShapeDtypeStruct((outer, rows_out, d), F32), jax.ShapeDtypeStruct((outer, rows_out, d), BF16),
                   jax.ShapeDtypeStruct((ne, b * nt * TM), F32), jax.ShapeDtypeStruct((ne, b * nt * TM), F32),
                   jax.ShapeDtypeStruct((b * nt, ne, LANES), I32), jax.ShapeDtypeStruct((ne, LANES), I32)],
        scratch_shapes=[pltpu.VMEM((ne, LANES), F32)],
        compiler_params=pltpu.CompilerParams(
            dimension_semantics=("arbitrary", "arbitrary"), vmem_limit_bytes=VMEM_LIMIT),
        name="merge_route",
    )(*arrays, mods, wb, w_out, g_post, g_ffn, w_rt, b_router)
    return h2.reshape(b, nt * TM, d), fx.reshape(b, nt * TM, d), wts, posm, meta, totals


def _route_tile(logits, bias):
    ne, tm = logits.shape
    gsz = ne // N_GROUPS
    neg = -jnp.inf

    scores = jax.nn.sigmoid(logits)
    sel = scores + bias

    jidx = lax.broadcasted_iota(I32, (gsz, tm), 0).astype(F32)
    gscore = []
    for g in range(N_GROUPS):
        v = sel[g * gsz:(g + 1) * gsz]
        m1 = jnp.max(v, axis=0, keepdims=True)
        first = jnp.min(jnp.where(v == m1, jidx, float(gsz)), axis=0, keepdims=True)
        m2 = jnp.max(jnp.where(jidx == first, neg, v), axis=0, keepdims=True)
        gscore.append(m1 + m2)
    masked = []
    for g in range(N_GROUPS):
        rank = jnp.zeros((1, tm), I32)
        for o in range(N_GROUPS):
            if o == g:
                continue
            ahead = gscore[o] > gscore[g]
            if o < g:
                ahead = ahead | (gscore[o] == gscore[g])
            rank = rank + ahead.astype(I32)
        masked.append(jnp.where(rank < TOPK_GROUPS, sel[g * gsz:(g + 1) * gsz], neg))
    cur = jnp.concatenate(masked, axis=0)

    eidx = lax.broadcasted_iota(I32, (ne, tm), 0).astype(F32)
    chosen = jnp.zeros((ne, tm), jnp.bool_)
    for _ in range(TOP_K):
        mx = jnp.max(cur, axis=0, keepdims=True)
        first = jnp.min(jnp.where(cur == mx, eidx, float(ne)), axis=0, keepdims=True)
        hit = eidx == first
        chosen = chosen | hit
        cur = jnp.where(hit, neg, cur)

    wsel = jnp.where(chosen, scores, 0.0)
    wts = wsel / jnp.sum(wsel, axis=0, keepdims=True) * ROUTED_SCALE

    cmask = chosen.astype(BF16)
    ti = lax.broadcasted_iota(I32, (tm, tm), 0)
    tj = lax.broadcasted_iota(I32, (tm, tm), 1)
    pos = _dot(cmask, (ti < tj).astype(BF16))
    ei = lax.broadcasted_iota(I32, (ne, ne), 0)
    ej = lax.broadcasted_iota(I32, (ne, ne), 1)
    cnt = jnp.sum(chosen.astype(F32), axis=1, keepdims=True)
    cpad = jnp.floor((cnt + (ALIGN - 1)) * (1.0 / ALIGN)) * ALIGN
    cpad_b = jnp.broadcast_to(cpad, (ne, LANES))
    loc_b = _dot((ej < ei).astype(BF16), cpad_b.astype(BF16))
    return wts, jnp.where(chosen, pos, -1.0), cpad_b, loc_b


WAIT_ROWS = 16 * ALIGN
START_UNROLL = 4


def _copy(local, remote, l0, r0, rows, sem, to_remote):
    lrows = local.at[pl.ds(l0, rows)]
    rrows = remote.at[pl.ds(r0, rows)]
    return pltpu.make_async_copy(lrows, rrows, sem) if to_remote else pltpu.make_async_copy(rrows, lrows, sem)


def _start_chunks(rem_ref, n, local, remote, sem, to_remote):
    def start(c):
        _copy(local, remote, pl.multiple_of(c * ALIGN, ALIGN), pl.multiple_of(rem_ref[0, 0, c], ALIGN), ALIGN,
              sem, to_remote).start()

    def per_group(g, carry):
        for u in range(START_UNROLL):
            start(g * START_UNROLL + u)
        return carry

    groups = n // START_UNROLL
    lax.fori_loop(0, groups, per_group, 0)
    lax.fori_loop(groups * START_UNROLL, n, lambda c, carry: (start(c), carry)[1], 0)


def _wait_chunks(n, local, remote, sem, to_remote):
    per = WAIT_ROWS // ALIGN
    lax.fori_loop(0, n // per, lambda i, c: (_copy(local, remote, 0, 0, WAIT_ROWS, sem, to_remote).wait(), c)[1], 0)
    lax.fori_loop(0, lax.rem(n, per), lambda i, c: (_copy(local, remote, 0, 0, ALIGN, sem, to_remote).wait(), c)[1], 0)


def _dispatch_kernel(n16_ref, tail_ref, tailn_ref, nused_ref, fx_ref, posm_ref, lohi_ref, rem_ref, xs_ref,
                     buf_ref, zero_ref, sem):
    t = pl.program_id(0)
    slot = lax.rem(t, 2)
    x = fx_ref[0]
    buf = buf_ref.at[slot]
    posm = posm_ref[...]
    m2 = jnp.concatenate([posm, jnp.zeros((LANES - posm.shape[0], posm.shape[1]), F32)], axis=0).astype(BF16)
    lo = lohi_ref[0, 0:1, :]
    hi = lohi_ref[0, 1:2, :]
    def sort_chunk(rc):
        r = (lax.broadcasted_iota(I32, (RCH, 1), 0) + rc * RCH).astype(F32)
        member = (r >= lo) & (r < hi)
        base = jnp.sum(jnp.where(member, lo, 0.0), axis=1, keepdims=True)
        onehot = (_dot(member.astype(BF16), m2) == r - base).astype(BF16)
        buf[rc * RCH:(rc + 1) * RCH, :] = _dot(onehot, x).astype(buf_ref.dtype)

    last_rc = RL // RCH - 1
    for rc in range(last_rc):
        sort_chunk(rc)
    pl.when(n16_ref[t] * ALIGN > last_rc * RCH)(lambda: sort_chunk(last_rc))

    @pl.when(t > 0)
    def _():
        _wait_chunks(n16_ref[t - 1], buf_ref.at[1 - slot], xs_ref, sem.at[0], True)

    _start_chunks(rem_ref, n16_ref[t], buf, xs_ref, sem.at[0], True)

    def tail(act):
        def go(cp):
            if act == 0:
                cp.start()
            else:
                cp.wait()

        def per_expert(e, carry):
            s0 = tail_ref[e]

            def per_chunk(j, c):
                go(pltpu.make_async_copy(
                    zero_ref.at[pl.ds(0, ALIGN)],
                    xs_ref.at[pl.ds(pl.multiple_of(s0 + j * ALIGN, ALIGN), ALIGN)], sem.at[1]))
                return c

            lax.fori_loop(0, tailn_ref[e], per_chunk, 0)
            return carry

        lax.fori_loop(0, N_EXPERTS, per_expert, 0)

        def per_block(j, c):
            go(pltpu.make_async_copy(zero_ref, xs_ref.at[pl.ds(pl.multiple_of(j * EBLK, EBLK), EBLK)], sem.at[1]))
            return c

        lax.fori_loop(nused_ref[0], xs_ref.shape[0] // EBLK, per_block, 0)

    @pl.when(t == pl.num_programs(0) - 1)
    def _():
        zero_ref[...] = jnp.zeros_like(zero_ref)
        tail(0)
        _wait_chunks(n16_ref[t], buf, xs_ref, sem.at[0], True)
        tail(1)


def _dispatch(fx, posm, lohi, tabs, rmax):
    b, rows, d = fx.shape
    nt = rows // TM
    ntiles = b * nt
    rem = tabs["rem"]
    return pl.pallas_call(
        _dispatch_kernel,
        grid_spec=pltpu.PrefetchScalarGridSpec(
            num_scalar_prefetch=4,
            grid=(ntiles,),
            in_specs=[
                pl.BlockSpec((1, TM, d), lambda t, *_: (t // nt, t % nt, 0)),
                pl.BlockSpec((posm.shape[0], TM), lambda t, *_: (0, t)),
                pl.BlockSpec((1,) + lohi.shape[1:], lambda t, *_: (t, 0, 0)),
                pl.BlockSpec((1,) + rem.shape[1:], lambda t, *_: (t, 0, 0), memory_space=pltpu.SMEM),
            ],
            out_specs=pl.BlockSpec(memory_space=pl.ANY),
            scratch_shapes=[pltpu.VMEM((2, RL, d), BF16), pltpu.VMEM((EBLK, d), BF16),
                            pltpu.SemaphoreType.DMA((2,))],
        ),
        out_shape=jax.ShapeDtypeStruct((rmax, d), BF16),
        compiler_params=pltpu.CompilerParams(dimension_semantics=("arbitrary",), vmem_limit_bytes=VMEM_LIMIT),
        name="dispatch",
    )(tabs["n16"], tabs["tail_start"], tabs["tail_n16"], tabs["n_used"], fx, posm, lohi, rem)


def _expert_kernel(blk_ref, nused_ref, x_ref, wg_ref, wu_ref, wd_ref, y_ref, wgu_bf, wd_bf):
    i = pl.program_id(0)

    @pl.when((i == 0) | (blk_ref[i] != blk_ref[jnp.maximum(i - 1, 0)]))
    def _():
        wgu_bf[:, :EXPERT_FF] = wg_ref[0].astype(BF16)
        wgu_bf[:, EXPERT_FF:] = wu_ref[0].astype(BF16)
        wd_bf[...] = wd_ref[0].astype(BF16)

    @pl.when(i < nused_ref[0])
    def _():
        gu = _dot(x_ref[...], wgu_bf[...])
        hmid = (_silu(gu[:, :EXPERT_FF]) * gu[:, EXPERT_FF:]).astype(BF16)
        y_ref[...] = _dot(hmid, wd_bf[...]).astype(y_ref.dtype)


def _experts(xs, blk_e, n_used, w_g, w_u, w_d, layer):
    rmax, d = xs.shape
    nblk = rmax // EBLK
    row = lambda i, blk, nu: (jnp.minimum(i, nu[0] - 1), 0)
    wsel = lambda i, blk, nu: (layer, blk[i], 0, 0)
    return pl.pallas_call(
        _expert_kernel,
        grid_spec=pltpu.PrefetchScalarGridSpec(
            num_scalar_prefetch=2,
            grid=(nblk,),
            in_specs=[
                pl.BlockSpec((EBLK, d), row),
                pl.BlockSpec((None, 1, d, EXPERT_FF), wsel),
                pl.BlockSpec((None, 1, d, EXPERT_FF), wsel),
                pl.BlockSpec((None, 1, EXPERT_FF, d), wsel),
            ],
            out_specs=pl.BlockSpec((EBLK, d), row),
            scratch_shapes=[pltpu.VMEM((d, 2 * EXPERT_FF), BF16), pltpu.VMEM((EXPERT_FF, d), BF16)],
        ),
        out_shape=jax.ShapeDtypeStruct((rmax, d), BF16),
        input_output_aliases={2: 0},
        compiler_params=pltpu.CompilerParams(dimension_semantics=("arbitrary",), vmem_limit_bytes=VMEM_LIMIT),
        name="experts",
    )(blk_e, n_used, xs, w_g, w_u, w_d)


def _combine_kernel(n16_ref, y_ref, posmt_ref, wt_ref, meta_ref, fx_ref, h_ref, mod_ref, wsgu_ref,
                    wsd_ref, gpost_ref, rem_ref, remn_ref, o_ref, buf_ref, sem):
    t = pl.program_id(0)
    slot = lax.rem(t, 2)
    buf = buf_ref.at[slot]

    @pl.when(t == 0)
    def _():
        buf_ref[...] = jnp.zeros_like(buf_ref)
        _start_chunks(rem_ref, n16_ref[t], buf, y_ref, sem.at[slot], False)

    @pl.when(t + 1 < pl.num_programs(0))
    def _():
        _start_chunks(remn_ref, n16_ref[t + 1], buf_ref.at[1 - slot], y_ref, sem.at[1 - slot], False)

    x = fx_ref[0]
    gu = _dot(x, wsgu_ref[...])
    ff = wsd_ref.shape[0]
    o_ref[0] = _dot((_silu(gu[:, :ff]) * gu[:, ff:]).astype(BF16), wsd_ref[...])

    _wait_chunks(n16_ref[t], buf, y_ref, sem.at[slot], False)

    posmt = posmt_ref[...].astype(BF16)
    wt = wt_ref[...].astype(BF16)
    lo = meta_ref[0, :, 1:2].astype(F32)
    hi = lo + meta_ref[0, :, 0:1].astype(F32)
    def unsort_chunk(rc):
        r = (lax.broadcasted_iota(I32, (1, RCH), 1) + rc * RCH).astype(F32)
        member = (r >= lo) & (r < hi)
        base = jnp.sum(jnp.where(member, lo, 0.0), axis=0, keepdims=True)
        mb = member.astype(BF16)
        pw = jnp.where(_dot(posmt, mb) == r - base, _dot(wt, mb), 0.0).astype(BF16)
        o_ref[0] += _dot(pw, buf[rc * RCH:(rc + 1) * RCH, :])

    last_rc = RL // RCH - 1
    for rc in range(last_rc):
        unsort_chunk(rc)
    pl.when(n16_ref[t] * ALIGN > last_rc * RCH)(lambda: unsort_chunk(last_rc))

    m = mod_ref[0, 0]
    o_ref[0] = h_ref[0] + m[5:6] * _rms(o_ref[0], gpost_ref[...])


def _combine(y, posm_t, wts_t, meta, fx, h2, mods, tabs, w_sgu, w_sd, g_post, n_lat):
    b, rows, d = fx.shape
    nt = rows // TM
    ntiles = b * nt
    nlt = n_lat // TM
    ne = posm_t.shape[1]
    rem = tabs["rem"]
    tile = lambda t, *_: (t // nt, t % nt, 0)
    return pl.pallas_call(
        _combine_kernel,
        grid_spec=pltpu.PrefetchScalarGridSpec(
            num_scalar_prefetch=1,
            grid=(ntiles,),
            in_specs=[
                pl.BlockSpec(memory_space=pl.ANY),
                pl.BlockSpec((TM, ne), lambda t, *_: (t, 0)),
                pl.BlockSpec((TM, ne), lambda t, *_: (t, 0)),
                pl.BlockSpec((1,) + meta.shape[1:], lambda t, *_: (t, 0, 0)),
                pl.BlockSpec((1, TM, d), tile),
                pl.BlockSpec((1, TM, d), tile),
                pl.BlockSpec((1, 1, 6, d), lambda t, *_: (t // nt, jnp.where(t % nt < nlt, 1, 0), 0, 0)),
                pl.BlockSpec(w_sgu.shape, lambda t, *_: (0, 0), pipeline_mode=pl.Buffered(1)),
                pl.BlockSpec(w_sd.shape, lambda t, *_: (0, 0), pipeline_mode=pl.Buffered(1)),
                pl.BlockSpec(g_post.shape, lambda t, *_: (0, 0), pipeline_mode=pl.Buffered(1)),
                pl.BlockSpec((1,) + rem.shape[1:], lambda t, *_: (t, 0, 0), memory_space=pltpu.SMEM),
                pl.BlockSpec((1,) + rem.shape[1:], lambda t, *_: (jnp.minimum(t + 1, ntiles - 1), 0, 0),
                             memory_space=pltpu.SMEM),
            ],
            out_specs=pl.BlockSpec((1, TM, d), tile),
            scratch_shapes=[pltpu.VMEM((2, RL, d), BF16), pltpu.SemaphoreType.DMA((2,))],
        ),
        out_shape=jax.ShapeDtypeStruct((b, rows, d), F32),
        compiler_params=pltpu.CompilerParams(dimension_semantics=("arbitrary",), vmem_limit_bytes=VMEM_LIMIT),
        name="combine",
    )(tabs["n16"], y, posm_t, wts_t, meta, fx, h2, mods, w_sgu, w_sd, g_post, rem, rem)


def _rope_tables(n_lat, n_ctx, rot_dim, lane_slots):
    rows = n_lat // GRID_W
    pos_row = jnp.repeat(jnp.arange(rows, dtype=F32), GRID_W)
    pos_col = jnp.tile(jnp.arange(GRID_W, dtype=F32), rows)
    d_axis = rot_dim // 2
    inv_freq = ROPE_BASE ** (-jnp.arange(0, d_axis, 2, dtype=F32) / d_axis)
    ar = pos_row[:, None] * inv_freq
    ac = pos_col[:, None] * inv_freq
    ang = jnp.concatenate([ar, ar, ac, ac], axis=1)
    qtr = rot_dim // 4
    first = (jnp.arange(rot_dim) // qtr) % 2 == 0
    cos, sin = jnp.cos(ang), jnp.sin(ang)
    s1 = jnp.where(first, -sin, 0.0)
    s2 = jnp.where(first, 0.0, sin)
    lc = n_lat + n_ctx
    c_t = jnp.ones((lc, LANES), F32)
    s1_t = jnp.zeros((lc, LANES), F32)
    s2_t = jnp.zeros((lc, LANES), F32)
    for lo in lane_slots:
        c_t = c_t.at[:n_lat, lo:lo + rot_dim].set(cos)
        s1_t = s1_t.at[:n_lat, lo:lo + rot_dim].set(s1)
        s2_t = s2_t.at[:n_lat, lo:lo + rot_dim].set(s2)
    return jnp.stack([c_t, s1_t, s2_t])


def _pack_layer_weights(w_in, w_uq, w_ukv):
    d = w_in.shape[0]
    kv0 = 0
    kr0 = kv0 + MLA_KV_RANK
    kd0 = kr0 + MLA_ROPE
    vd0 = kd0 + 512
    q0 = vd0 + 512
    qd0 = q0 + MLA_Q_RANK
    p0 = qd0 + 512
    g0 = p0 + 512
    kr = jnp.zeros((d, LANES), w_in.dtype).at[:, MLA_NOPE:MLA_NOPE + MLA_ROPE].set(w_in[:, kr0:kd0])
    w1 = jnp.concatenate([w_in[:, kv0:kr0], kr, w_in[:, kd0:vd0], w_in[:, vd0:q0], w_in[:, q0:qd0],
                          w_in[:, qd0:p0], w_in[:, p0:g0], w_in[:, g0:]], axis=1).astype(BF16)
    ukv = w_ukv.reshape(MLA_KV_RANK, MLA_HEADS, MLA_NOPE + MLA_V)
    uk = jnp.pad(ukv[:, :, :MLA_NOPE], ((0, 0), (0, 0), (0, LANES - MLA_NOPE))).reshape(MLA_KV_RANK, -1)
    uv = ukv[:, :, MLA_NOPE:].reshape(MLA_KV_RANK, -1)
    w_ukv_p = jnp.concatenate([uk, uv], axis=1).astype(BF16)
    uq = w_uq.reshape(MLA_Q_RANK, MLA_HEADS, MLA_NOPE + MLA_ROPE)
    w_uq_p = jnp.pad(uq, ((0, 0), (0, 0), (0, LANES - MLA_NOPE - MLA_ROPE))).reshape(MLA_Q_RANK, -1).astype(BF16)
    return w1, w_ukv_p, w_uq_p


def _routing_tables(meta, totals, rmax):
    cpad, loc, gpos = meta[:, :, 0], meta[:, :, 1], meta[:, :, 2]
    tot = totals[:, 0]
    reg = (tot + EBLK - 1) // EBLK * EBLK
    reg_end = jnp.cumsum(reg)
    reg_start = reg_end - reg
    n_used = jnp.maximum(reg_end[-1:] // EBLK, 1).astype(I32)
    blk_start = jnp.arange(rmax // EBLK, dtype=I32) * EBLK
    blk_e = jnp.minimum(jnp.sum(reg_end[None, :] <= blk_start[:, None], axis=1), N_EXPERTS - 1).astype(I32)
    ntiles, ne = loc.shape
    lohi = jnp.zeros((ntiles, 8, LANES), F32).at[:, 0, :].set(float(RL)).at[:, 0, :ne].set(loc.astype(F32))
    lohi = lohi.at[:, 1, :ne].set((loc + cpad).astype(F32))
    chunk = jnp.arange(RL // ALIGN, dtype=I32)[None, :, None]
    seg_lo = (loc // ALIGN)[:, None, :]
    seg_end = (loc + cpad) // ALIGN
    inside = (seg_lo <= chunk) & (chunk < seg_end[:, None, :])
    shift = jnp.sum(jnp.where(inside, (reg_start[None, :] + gpos - loc)[:, None, :], 0), axis=-1)
    rem = (shift + chunk[:, :, 0] * ALIGN).astype(I32)[:, None, :]
    return dict(
        n16=seg_end[:, -1].astype(I32), rem=rem,
        tail_start=(reg_start + tot).astype(I32), tail_n16=((reg - tot) // ALIGN).astype(I32),
        n_used=n_used), blk_e, lohi


def kernel(x, c, ctx, c_ctx, w_mod, b_mod, g_pre_mix, g_post_mix, g_pre_ffn, g_post_ffn, w_in, g_cq, w_uq, g_ckv, w_ukv, w_pool, pool_scale, lam_q1, lam_k1, lam_q2, lam_k2, g_subln, w_b_mla, w_b_pool, w_b_diff, w_out, w_router, b_router, w_e_gate, w_e_up, w_e_down, w_s_gate, w_s_up, w_s_down):
    b, n_lat, d = x.shape
    n_ctx = ctx.shape[1]
    depth = w_mod.shape[0]
    assert n_ctx == TM and n_lat % QROWS == 0 and n_lat % GRID_W == 0
    lc = n_lat + n_ctx
    row2 = lambda v: v.reshape(1, -1)

    rows = (b + 1 + 7) // 8 * 8
    cvec = jnp.zeros((rows, d), F32).at[:b].set(c).at[b].set(c_ctx)
    mod_all = _modulation(cvec, w_mod, b_mod).reshape(depth, rows, 6, d)

    tab_m = _rope_tables(n_lat, n_ctx, MLA_ROPE, (MLA_NOPE,))
    tab_d = _rope_tables(n_lat, n_ctx, DIFF_HALF, (0, DIFF_HALF))

    h = jnp.concatenate([x, ctx], axis=1)
    for l in range(depth):
        last = l == depth - 1
        lam_init = 0.8 - 0.6 * math.exp(-0.3 * l)
        mods = jnp.stack([jnp.broadcast_to(mod_all[l, b], (b, 6, d)), mod_all[l, :b]], axis=1)
        w1, w_ukv_p, w_uq_p = _pack_layer_weights(w_in[l], w_uq[l], w_ukv[l])

        qm, km, vm, qd, kd, vd, pool_in, gates = _project(
            h, mods, row2(g_pre_mix[l]), w1, row2(g_ckv[l]), w_ukv_p, row2(g_cq[l]), w_uq_p, tab_m, tab_d, n_lat)
        o_mla = _attention(
            functools.partial(_mla_kernel, n_lat=n_lat, has_ctx=not last), "mla_attention", qm, km, vm, (),
            [pltpu.VMEM((2, lc, 2 * MLA_V), BF16)], n_lat, not last, MLA_HEADS // 2, 2 * LANES, 2 * MLA_V, 2 * MLA_V)
        lam_p = jnp.stack([lam_q1[l], lam_k1[l], lam_q2[l], lam_k2[l]])
        o_diff = _attention(
            functools.partial(_diff_kernel, n_lat=n_lat, has_ctx=not last, lam_init=lam_init), "diff_attention",
            qd, kd, vd, (lam_p, row2(g_subln[l])), [], n_lat, not last, DIFF_HEADS, LANES, DIFF_V, DIFF_V)
        segments = ((0, n_lat),) if last else ((0, n_lat), (n_lat, n_ctx))
        o_pool = _pool_mix(pool_in, w_pool[l].astype(BF16), row2(pool_scale[l]), segments)

        nt = (n_lat if last else lc) // TM
        wb = jnp.stack([w_b_mla[l], w_b_pool[l], w_b_diff[l]]).astype(BF16)
        h2, fx, wts, posm, meta, totals = _merge(
            o_mla, o_diff, o_pool, gates, h, mods, wb, w_out[l].astype(BF16), row2(g_post_mix[l]),
            row2(g_pre_ffn[l]), w_router[l].T, b_router[l].reshape(-1, 1), n_lat, nt)
        rmax = meta.shape[0] * RL + N_EXPERTS * EBLK
        tabs, blk_e, lohi = _routing_tables(meta, totals, rmax)
        xs = _dispatch(fx, posm, lohi, tabs, rmax)
        ys = _experts(xs, blk_e, tabs["n_used"], w_e_gate, w_e_up, w_e_down, l)
        w_sgu = jnp.concatenate([w_s_gate[l], w_s_up[l]], axis=-1).astype(BF16)
        h = _combine(ys, posm.T, wts.T, meta, fx, h2, mods, tabs, w_sgu, w_s_down[l].astype(BF16),
                     row2(g_post_ffn[l]), n_lat)
    return h
```

```python
import functools
import math

import jax
import jax.numpy as jnp
from jax import lax
from jax.experimental import pallas as pl
from jax.experimental.pallas import tpu as pltpu

F32 = jnp.float32
BF16 = jnp.bfloat16
I32 = jnp.int32

EPS = 1e-6
ROPE_BASE = 10000.0
GRID_W = 64

MLA_HEADS = 8
MLA_NOPE = 64
MLA_ROPE = 32
MLA_V = 64
MLA_Q_RANK = 384
MLA_KV_RANK = 256
POOL_WINDOWS = (2, 4, 8, 16)
POOL_GROUP_W = 128
DIFF_HEADS = 4
DIFF_HALF = 64
DIFF_V = 128
N_EXPERTS = 64
TOP_K = 8
N_GROUPS = 8
TOPK_GROUPS = 4
EXPERT_FF = 256
ROUTED_SCALE = 2.5

LANES = 128
TM = 256
QROWS = 256
ALIGN = 16
EBLK = 1024
RL = TM * TOP_K + N_EXPERTS * ALIGN
RCH = 512
VMEM_LIMIT = 56 * 1024 * 1024
LOG2E = 1.4426950408889634

C_CKV = 0
C_KR = C_CKV + MLA_KV_RANK
C_KD = C_KR + LANES
C_VD = C_KD + 512
C_CQ = C_VD + 512
C_QD = C_CQ + MLA_Q_RANK
C_POOL = C_QD + 512
C_GATE = C_POOL + 512


def _rms(x, g):
    ms = jnp.mean(x * x, axis=-1, keepdims=True)
    return x * lax.rsqrt(ms + EPS) * g


def _dot(a, b):
    return jnp.dot(a, b, preferred_element_type=F32)


def _dot_nt(a, b):
    return lax.dot_general(a, b, (((1,), (1,)), ((), ())), preferred_element_type=F32)


def _split_bf16(x):
    hi = x.astype(BF16)
    lo = (x - hi.astype(F32)).astype(BF16)
    return hi, lo


def _silu(x):
    return x * jax.nn.sigmoid(x)


def _rope(x, tab, quarter):
    w = x.shape[-1]
    rep = w // LANES
    c, s1, s2 = (jnp.tile(t, (1, rep)) if rep > 1 else t for t in tab)
    return x * c + pltpu.roll(x, w - quarter, 1) * s1 + pltpu.roll(x, quarter, 1) * s2


def _const_spec(shape):
    nd = len(shape)
    return pl.BlockSpec(shape, lambda *_: (0,) * nd, pipeline_mode=pl.Buffered(1))


def _mod_kernel(a_ref, w_ref, b_ref, o_ref):
    a = _silu(a_ref[...])
    ahi, alo = _split_bf16(a)
    whi, wlo = _split_bf16(w_ref[...])
    o_ref[...] = _dot(ahi, whi) + _dot(alo, whi) + _dot(ahi, wlo) + b_ref[...]


def _modulation(cvec, w_mod, b_mod):
    nl, d, d6 = w_mod.shape
    rows = cvec.shape[0]
    nc = d6 // d
    return pl.pallas_call(
        _mod_kernel,
        grid=(nl, nc),
        in_specs=[
            pl.BlockSpec((rows, d), lambda l, j: (0, 0)),
            pl.BlockSpec((None, d, d), lambda l, j: (l, 0, j)),
            pl.BlockSpec((None, 1, d), lambda l, j: (l, 0, j)),
        ],
        out_specs=pl.BlockSpec((None, rows, d), lambda l, j: (l, 0, j)),
        out_shape=jax.ShapeDtypeStruct((nl, rows, d6), F32),
        name="modulation",
    )(cvec, w_mod, b_mod.reshape(nl, 1, d6))


PROJ_GROUP = 2


def _proj_kernel(h_ref, mod_ref, gpre_ref, w1_ref, gckv_ref, wukv_ref, gcq_ref, wuq_ref, tm_ref, td_ref,
                 qm_ref, km_ref, vm_ref, qd_ref, kd_ref, vd_ref, pool_ref, gate_ref, *, tiles_per_sample,
                 n_lat_tiles, groups):
    d = h_ref.shape[-1]
    hx, tabm, tabd = [], [], []
    for g in range(groups):
        tile = pl.program_id(0) * groups + g
        sample = tile // tiles_per_sample
        pos_tile = tile - sample * tiles_per_sample
        m = jnp.where(pos_tile < n_lat_tiles, mod_ref[sample, 1], mod_ref[sample, 0])
        pos = pl.ds(pl.multiple_of(pos_tile * TM, TM), TM)
        tabm.append([tm_ref[k, pos, :] for k in range(3)])
        tabd.append([td_ref[k, pos, :] for k in range(3)])
        hx.append((_rms(h_ref[g * TM:(g + 1) * TM, :], gpre_ref[...]) * (1.0 + m[1:2]) + m[0:1]).astype(BF16))
    hx = jnp.concatenate(hx, axis=0)
    tabm = tuple(jnp.concatenate([tg[k] for tg in tabm], axis=0) for k in range(3))
    tabd = tuple(jnp.concatenate([tg[k] for tg in tabd], axis=0) for k in range(3))

    ckv = _dot(hx, w1_ref[:, C_CKV:C_KR])
    kv = _dot(_rms(ckv, gckv_ref[...]).astype(BF16), wukv_ref[...])
    kr = _rope(_dot(hx, w1_ref[:, C_KR:C_KD]), tabm, MLA_ROPE // 4)
    nk = MLA_HEADS * LANES
    km_ref[...] = (kv[:, :nk] + jnp.tile(kr, (1, MLA_HEADS))).astype(BF16)
    vm_ref[...] = kv[:, nk:].astype(BF16)

    kd_ref[...] = _rope(_dot(hx, w1_ref[:, C_KD:C_VD]), tabd, DIFF_HALF // 4).astype(BF16)
    vd_ref[...] = _dot(hx, w1_ref[:, C_VD:C_CQ]).astype(BF16)

    cq = _dot(hx, w1_ref[:, C_CQ:C_QD])
    q = _dot(_rms(cq, gcq_ref[...]).astype(BF16), wuq_ref[...])
    qm_scale = LOG2E / math.sqrt(MLA_NOPE + MLA_ROPE)
    qm_ref[...] = (_rope(q, tabm, MLA_ROPE // 4) * qm_scale).astype(BF16)
    qd_scale = LOG2E / math.sqrt(DIFF_HALF)
    qd_ref[...] = (_rope(_dot(hx, w1_ref[:, C_QD:C_POOL]), tabd, DIFF_HALF // 4) * qd_scale).astype(BF16)
    pool_ref[...] = _dot(hx, w1_ref[:, C_POOL:C_GATE]).astype(BF16)
    for j in range(3):
        lo = C_GATE + j * d
        gate_ref[:, j * d:(j + 1) * d] = jax.nn.sigmoid(_dot(hx, w1_ref[:, lo:lo + d])).astype(BF16)


def _project(h, mods, g_pre, w1, g_ckv, w_ukv, g_cq, w_uq, tab_m, tab_d, n_lat):
    b, lc, d = h.shape
    nt = lc // TM
    groups = PROJ_GROUP if (b * nt) % PROJ_GROUP == 0 else 1
    tile = lambda w: pl.BlockSpec((groups * TM, w), lambda i: (i, 0))
    widths = (MLA_HEADS * LANES, MLA_HEADS * LANES, MLA_HEADS * MLA_V, 512, 512, 512, 512, 3 * d)
    outs = pl.pallas_call(
        functools.partial(_proj_kernel, tiles_per_sample=nt, n_lat_tiles=n_lat // TM, groups=groups),
        grid=(b * nt // groups,),
        in_specs=[tile(d)] + [_const_spec(a.shape) for a in (mods, g_pre, w1, g_ckv, w_ukv, g_cq, w_uq, tab_m, tab_d)],
        out_specs=[tile(w) for w in widths],
        out_shape=[jax.ShapeDtypeStruct((b * lc, w), BF16) for w in widths],
        compiler_params=pltpu.CompilerParams(dimension_semantics=("parallel",), vmem_limit_bytes=VMEM_LIMIT),
        name="project",
    )(h.reshape(b * lc, d), mods, g_pre, w1, g_ckv, w_ukv, g_cq, w_uq, tab_m, tab_d)
    return [o.reshape(b, lc, -1) for o in outs]


def _exp_scores(q, k):
    s = _dot_nt(q, k)
    return jnp.exp2(s - jnp.max(s, axis=-1, keepdims=True))


def _query_groups(n_lat, lc, has_ctx):
    groups = [(r0, QROWS, 0) for r0 in range(0, n_lat, QROWS)]
    return groups + ([(n_lat, lc - n_lat, n_lat)] if has_ctx else [])


def _mla_kernel(q_ref, k_ref, v_ref, o_ref, vext_ref, *, n_lat, has_ctx):
    lc = k_ref.shape[1]
    lane = lax.broadcasted_iota(I32, (1, LANES), 1)
    v = v_ref[0]
    one = jnp.ones_like(v)
    vext_ref[0] = jnp.where(lane < MLA_V, v, one)
    vext_ref[1] = jnp.where(lane < MLA_V, one, v)
    for r0, rows, k0 in _query_groups(n_lat, lc, has_ctx):
        pv = []
        for j in range(2):
            e = _exp_scores(q_ref[0, r0:r0 + rows, j * LANES:(j + 1) * LANES],
                            k_ref[0, k0:lc, j * LANES:(j + 1) * LANES])
            pv.append(_dot(e.astype(BF16), vext_ref[j, k0:lc, :]))
        o0 = pv[0] * (1.0 / pv[0][:, MLA_V:MLA_V + 1])
        o1 = pv[1] * (1.0 / pv[1][:, 0:1])
        o_ref[0, r0:r0 + rows, :] = jnp.where(lane < MLA_V, o0, o1).astype(o_ref.dtype)


def _diff_kernel(q_ref, k_ref, v_ref, lam_ref, g_ref, o_ref, *, n_lat, has_ctx, lam_init):
    lc = k_ref.shape[1]
    lane = lax.broadcasted_iota(I32, (1, LANES), 1)
    lp = lam_ref[...]
    lam = (jnp.exp(jnp.sum(lp[0:1] * lp[1:2], axis=-1, keepdims=True))
           - jnp.exp(jnp.sum(lp[2:3] * lp[3:4], axis=-1, keepdims=True)) + lam_init)
    for r0, rows, k0 in _query_groups(n_lat, lc, has_ctx):
        q = q_ref[0, r0:r0 + rows, :]
        k = k_ref[0, k0:lc, :]
        v = v_ref[0, k0:lc, :]
        zero = jnp.zeros_like(q)
        e0 = _exp_scores(jnp.where(lane < DIFF_HALF, q, zero), k)
        e1 = _exp_scores(jnp.where(lane < DIFF_HALF, zero, q), k)
        o = (_dot(e0.astype(BF16), v) * (1.0 / jnp.sum(e0, axis=-1, keepdims=True))
             - _dot(e1.astype(BF16), v) * (lam / jnp.sum(e1, axis=-1, keepdims=True)))
        o_ref[0, r0:r0 + rows, :] = (_rms(o, g_ref[...]) * (1.0 - lam_init)).astype(o_ref.dtype)


def _attention(body, name, q, k, v, extras, scratch, n_lat, has_ctx, heads, wq, wv, wo):
    b, lc, _ = q.shape
    rows = lc if has_ctx else n_lat
    full = lambda n, w: pl.BlockSpec((1, n, w), lambda bi, h: (bi, 0, h))
    small = [pl.BlockSpec(a.shape, lambda bi, h: (0, 0)) for a in extras]
    return pl.pallas_call(
        body,
        grid=(b, heads),
        in_specs=[full(rows, wq), full(lc, wq), full(lc, wv)] + small,
        out_specs=full(rows, wo),
        out_shape=jax.ShapeDtypeStruct((b, rows, heads * wo), BF16),
        scratch_shapes=scratch,
        compiler_params=pltpu.CompilerParams(
            dimension_semantics=("parallel", "parallel"), vmem_limit_bytes=VMEM_LIMIT),
        name=name,
    )(q, k, v, *extras)


POOL_HALO = 16


def _pool_kernel(u_ref, w_ref, s_ref, o_ref, pad_ref, *, segments):
    for s0, n in segments:
        t = lax.broadcasted_iota(I32, (n, 1), 0)
        for g, w in enumerate(POOL_WINDOWS):
            cols = slice(g * POOL_GROUP_W, (g + 1) * POOL_GROUP_W)
            xg = u_ref[0, s0:s0 + n, cols].astype(F32)
            zeros = jnp.zeros((POOL_HALO, POOL_GROUP_W), F32)
            pad_ref[0:POOL_HALO, :] = zeros
            pad_ref[POOL_HALO + n:2 * POOL_HALO + n, :] = zeros
            pad_ref[POOL_HALO:POOL_HALO + n, :] = xg
            acc = pad_ref[POOL_HALO - w // 2:POOL_HALO - w // 2 + n, :]
            for j in range(1 - w // 2, w // 2):
                acc = acc + pad_ref[POOL_HALO + j:POOL_HALO + j + n, :]
            cnt = (jnp.minimum(t + w // 2, n) - jnp.maximum(t - w // 2, 0)).astype(F32)
            pooled = (acc / cnt - xg).astype(BF16)
            o_ref[0, s0:s0 + n, cols] = (_dot(pooled, w_ref[g]) * s_ref[:, cols]).astype(o_ref.dtype)


def _pool_mix(pool_in, w_pool, pool_scale, segments):
    b, lc, pw = pool_in.shape
    nmax = max(n for _, n in segments)
    lo = sum(n for _, n in segments)
    return pl.pallas_call(
        functools.partial(_pool_kernel, segments=segments),
        grid=(b,),
        in_specs=[
            pl.BlockSpec((1, lc, pw), lambda bi: (bi, 0, 0)),
            pl.BlockSpec(w_pool.shape, lambda bi: (0, 0, 0)),
            pl.BlockSpec(pool_scale.shape, lambda bi: (0, 0)),
        ],
        out_specs=pl.BlockSpec((1, lo, pw), lambda bi: (bi, 0, 0)),
        out_shape=jax.ShapeDtypeStruct((b, lo, pw), BF16),
        scratch_shapes=[pltpu.VMEM((nmax + 2 * POOL_HALO, POOL_GROUP_W), F32)],
        compiler_params=pltpu.CompilerParams(dimension_semantics=("parallel",), vmem_limit_bytes=VMEM_LIMIT),
        name="pool_mix",
    )(pool_in, w_pool, pool_scale)


MERGE_GROUP = 2


def _merge_kernel(om_ref, od_ref, op_ref, gate_ref, h_ref, mod_ref, wb_ref, wo_ref, gpost_ref, gffn_ref, wr_ref,
                  br_ref, h2_ref, fx_ref, wts_ref, posm_ref, meta_ref, tot_ref, carry_ref, *, tiles_per_sample,
                  n_lat_tiles, groups):
    d = h_ref.shape[-1]

    @pl.when((pl.program_id(0) == 0) & (pl.program_id(1) == 0))
    def _():
        carry_ref[...] = jnp.zeros_like(carry_ref)

    merged = (gate_ref[0, :, 0:d].astype(F32) * _dot(om_ref[0], wb_ref[0])
              + gate_ref[0, :, d:2 * d].astype(F32) * _dot(op_ref[0], wb_ref[1])
              + gate_ref[0, :, 2 * d:3 * d].astype(F32) * _dot(od_ref[0], wb_ref[2]))
    y = _dot(merged.astype(BF16), wo_ref[...])
    step = pl.program_id(0) * pl.num_programs(1) + pl.program_id(1)
    his, los = [], []
    for g in range(groups):
        rows = slice(g * TM, (g + 1) * TM)
        tile = step * groups + g
        sample = tile // tiles_per_sample
        m = jnp.where(tile - sample * tiles_per_sample < n_lat_tiles, mod_ref[sample, 1], mod_ref[sample, 0])
        h2 = h_ref[0, rows, :] + m[2:3] * _rms(y[rows], gpost_ref[...])
        h2_ref[0, rows, :] = h2
        fhi, flo = _split_bf16(_rms(h2, gffn_ref[...]) * (1.0 + m[4:5]) + m[3:4])
        fx_ref[0, rows, :] = fhi
        his.append(fhi)
        los.append(flo)
    fhi = jnp.concatenate(his, axis=0)
    flo = jnp.concatenate(los, axis=0)
    whi, wlo = _split_bf16(wr_ref[...])
    logits = _dot_nt(whi, fhi) + _dot_nt(whi, flo) + _dot_nt(wlo, fhi)

    lane = lax.broadcasted_iota(I32, carry_ref.shape, 1)
    for g in range(groups):
        cols = slice(g * TM, (g + 1) * TM)
        wts, posm, cpad_b, loc_b = _route_tile(logits[:, cols], br_ref[...])
        wts_ref[:, cols] = wts
        posm_ref[:, cols] = posm
        gpos = carry_ref[...]
        meta_ref[g] = jnp.where(lane == 0, cpad_b, jnp.where(lane == 1, loc_b, gpos)).astype(I32)
        carry_ref[...] = gpos + cpad_b
    tot_ref[...] = carry_ref[...].astype(I32)


def _merge(o_mla, o_diff, o_pool, gates, h, mods, wb, w_out, g_post, g_ffn, w_rt, b_router, n_lat, nt):
    b, lc, d = h.shape
    ne = w_rt.shape[0]
    groups = MERGE_GROUP
    tg = groups * TM
    if nt % groups == 0:
        outer, steps, rows_out = b, nt // groups, nt * TM
        arrays = [o_mla, o_diff, o_pool, gates, h]
    else:
        assert nt * TM == lc and (b * nt) % groups == 0 and all(a.shape[1] == lc for a in (o_mla, o_diff, o_pool))
        outer, steps, rows_out = 1, b * nt // groups, b * lc
        arrays = [a.reshape(1, b * lc, a.shape[-1]) for a in (o_mla, o_diff, o_pool, gates, h)]
    tile = lambda w: pl.BlockSpec((1, tg, w), lambda o, i: (o, i, 0))
    tokens = pl.BlockSpec((ne, tg), lambda o, i: (0, o * steps + i))
    h2, fx, wts, posm, meta, totals = pl.pallas_call(
        functools.partial(_merge_kernel, tiles_per_sample=nt, n_lat_tiles=n_lat // TM, groups=groups),
        grid=(outer, steps),
        in_specs=[tile(a.shape[-1]) for a in arrays] + [
            _const_spec(a.shape) for a in (mods, wb, w_out, g_post, g_ffn, w_rt, b_router)],
        out_specs=[tile(d), tile(d), tokens, tokens,
                   pl.BlockSpec((groups, ne, LANES), lambda o, i: (o * steps + i, 0, 0)),
                   pl.BlockSpec((ne, LANES), lambda o, i: (0, 0))],
        out_shape=[jax.ShapeDtypeStruct((outer, rows_out, d), F32), jax.ShapeDtypeStruct((outer, rows_out, d), BF16),
                   jax.ShapeDtypeStruct((ne, b * nt * TM), F32), jax.ShapeDtypeStruct((ne, b * nt * TM), F32),
                   jax.ShapeDtypeStruct((b * nt, ne, LANES), I32), jax.ShapeDtypeStruct((ne, LANES), I32)],
        scratch_shapes=[pltpu.VMEM((ne, LANES), F32)],
        compiler_params=pltpu.CompilerParams(
            dimension_semantics=("arbitrary", "arbitrary"), vmem_limit_bytes=VMEM_LIMIT),
        name="merge_route",
    )(*arrays, mods, wb, w_out, g_post, g_ffn, w_rt, b_router)
    return h2.reshape(b, nt * TM, d), fx.reshape(b, nt * TM, d), wts, posm, meta, totals


def _route_tile(logits, bias):
    ne, tm = logits.shape
    gsz = ne // N_GROUPS
    neg = -jnp.inf

    scores = jax.nn.sigmoid(logits)
    sel = scores + bias

    jidx = lax.broadcasted_iota(I32, (gsz, tm), 0).astype(F32)
    gscore = []
    for g in range(N_GROUPS):
        v = sel[g * gsz:(g + 1) * gsz]
        m1 = jnp.max(v, axis=0, keepdims=True)
        first = jnp.min(jnp.where(v == m1, jidx, float(gsz)), axis=0, keepdims=True)
        m2 = jnp.max(jnp.where(jidx == first, neg, v), axis=0, keepdims=True)
        gscore.append(m1 + m2)
    masked = []
    for g in range(N_GROUPS):
        rank = jnp.zeros((1, tm), I32)
        for o in range(N_GROUPS):
            if o == g:
                continue
            ahead = gscore[o] > gscore[g]
            if o < g:
                ahead = ahead | (gscore[o] == gscore[g])
            rank = rank + ahead.astype(I32)
        masked.append(jnp.where(rank < TOPK_GROUPS, sel[g * gsz:(g + 1) * gsz], neg))
    cur = jnp.concatenate(masked, axis=0)

    eidx = lax.broadcasted_iota(I32, (ne, tm), 0).astype(F32)
    chosen = jnp.zeros((ne, tm), jnp.bool_)
    for _ in range(TOP_K):
        mx = jnp.max(cur, axis=0, keepdims=True)
        first = jnp.min(jnp.where(cur == mx, eidx, float(ne)), axis=0, keepdims=True)
        hit = eidx == first
        chosen = chosen | hit
        cur = jnp.where(hit, neg, cur)

    wsel = jnp.where(chosen, scores, 0.0)
    wts = wsel / jnp.sum(wsel, axis=0, keepdims=True) * ROUTED_SCALE

    cmask = chosen.astype(BF16)
    ti = lax.broadcasted_iota(I32, (tm, tm), 0)
    tj = lax.broadcasted_iota(I32, (tm, tm), 1)
    pos = _dot(cmask, (ti < tj).astype(BF16))
    ei = lax.broadcasted_iota(I32, (ne, ne), 0)
    ej = lax.broadcasted_iota(I32, (ne, ne), 1)
    cnt = jnp.sum(chosen.astype(F32), axis=1, keepdims=True)
    cpad = jnp.floor((cnt + (ALIGN - 1)) * (1.0 / ALIGN)) * ALIGN
    cpad_b = jnp.broadcast_to(cpad, (ne, LANES))
    loc_b = _dot((ej < ei).astype(BF16), cpad_b.astype(BF16))
    return wts, jnp.where(chosen, pos, -1.0), cpad_b, loc_b


WAIT_ROWS = 16 * ALIGN
START_UNROLL = 4


def _copy(local, remote, l0, r0, rows, sem, to_remote):
    lrows = local.at[pl.ds(l0, rows)]
    rrows = remote.at[pl.ds(r0, rows)]
    return pltpu.make_async_copy(lrows, rrows, sem) if to_remote else pltpu.make_async_copy(rrows, lrows, sem)


def _start_chunks(rem_ref, n, local, remote, sem, to_remote):
    def start(c):
        _copy(local, remote, pl.multiple_of(c * ALIGN, ALIGN), pl.multiple_of(rem_ref[0, 0, c], ALIGN), ALIGN,
              sem, to_remote).start()

    def per_group(g, carry):
        for u in range(START_UNROLL):
            start(g * START_UNROLL + u)
        return carry

    groups = n // START_UNROLL
    lax.fori_loop(0, groups, per_group, 0)
    lax.fori_loop(groups * START_UNROLL, n, lambda c, carry: (start(c), carry)[1], 0)


def _wait_chunks(n, local, remote, sem, to_remote):
    per = WAIT_ROWS // ALIGN
    lax.fori_loop(0, n // per, lambda i, c: (_copy(local, remote, 0, 0, WAIT_ROWS, sem, to_remote).wait(), c)[1], 0)
    lax.fori_loop(0, lax.rem(n, per), lambda i, c: (_copy(local, remote, 0, 0, ALIGN, sem, to_remote).wait(), c)[1], 0)


def _dispatch_kernel(n16_ref, tail_ref, tailn_ref, nused_ref, fx_ref, posm_ref, lohi_ref, rem_ref, xs_ref,
                     buf_ref, zero_ref, sem):
    t = pl.program_id(0)
    slot = lax.rem(t, 2)
    x = fx_ref[0]
    buf = buf_ref.at[slot]
    posm = posm_ref[...]
    m2 = jnp.concatenate([posm, jnp.zeros((LANES - posm.shape[0], posm.shape[1]), F32)], axis=0).astype(BF16)
    lo = lohi_ref[0, 0:1, :]
    hi = lohi_ref[0, 1:2, :]
    def sort_chunk(rc):
        r = (lax.broadcasted_iota(I32, (RCH, 1), 0) + rc * RCH).astype(F32)
        member = (r >= lo) & (r < hi)
        base = jnp.sum(jnp.where(member, lo, 0.0), axis=1, keepdims=True)
        onehot = (_dot(member.astype(BF16), m2) == r - base).astype(BF16)
        buf[rc * RCH:(rc + 1) * RCH, :] = _dot(onehot, x).astype(buf_ref.dtype)

    last_rc = RL // RCH - 1
    for rc in range(last_rc):
        sort_chunk(rc)
    pl.when(n16_ref[t] * ALIGN > last_rc * RCH)(lambda: sort_chunk(last_rc))

    _start_chunks(rem_ref, n16_ref[t], buf, xs_ref, sem.at[slot], True)

    @pl.when(t > 0)
    def _():
        _wait_chunks(n16_ref[t - 1], buf_ref.at[1 - slot], xs_ref, sem.at[1 - slot], True)

    def tail(act):
        def go(cp):
            if act == 0:
                cp.start(priority=1)
            else:
                cp.wait()

        def per_expert(e, carry):
            s0 = tail_ref[e]

            def per_chunk(j, c):
                go(pltpu.make_async_copy(
                    zero_ref.at[pl.ds(0, ALIGN)],
                    xs_ref.at[pl.ds(pl.multiple_of(s0 + j * ALIGN, ALIGN), ALIGN)], sem.at[2]))
                return c

            lax.fori_loop(0, tailn_ref[e], per_chunk, 0)
            return carry

        lax.fori_loop(0, N_EXPERTS, per_expert, 0)

        def per_block(j, c):
            go(pltpu.make_async_copy(zero_ref, xs_ref.at[pl.ds(pl.multiple_of(j * EBLK, EBLK), EBLK)], sem.at[2]))
            return c

        lax.fori_loop(nused_ref[0], xs_ref.shape[0] // EBLK, per_block, 0)

    @pl.when(t == 0)
    def _():
        zero_ref[...] = jnp.zeros_like(zero_ref)
        tail(0)

    @pl.when(t == pl.num_programs(0) - 1)
    def _():
        _wait_chunks(n16_ref[t], buf, xs_ref, sem.at[slot], True)
        tail(1)


def _dispatch(fx, posm, lohi, tabs, rmax):
    b, rows, d = fx.shape
    nt = rows // TM
    ntiles = b * nt
    rem = tabs["rem"]
    return pl.pallas_call(
        _dispatch_kernel,
        grid_spec=pltpu.PrefetchScalarGridSpec(
            num_scalar_prefetch=4,
            grid=(ntiles,),
            in_specs=[
                pl.BlockSpec((1, TM, d), lambda t, *_: (t // nt, t % nt, 0)),
                pl.BlockSpec((posm.shape[0], TM), lambda t, *_: (0, t)),
                pl.BlockSpec((1,) + lohi.shape[1:], lambda t, *_: (t, 0, 0)),
                pl.BlockSpec((1,) + rem.shape[1:], lambda t, *_: (t, 0, 0), memory_space=pltpu.SMEM),
            ],
            out_specs=pl.BlockSpec(memory_space=pl.ANY),
            scratch_shapes=[pltpu.VMEM((2, RL, d), BF16), pltpu.VMEM((EBLK, d), BF16),
                            pltpu.SemaphoreType.DMA((3,))],
        ),
        out_shape=jax.ShapeDtypeStruct((rmax, d), BF16),
        compiler_params=pltpu.CompilerParams(dimension_semantics=("arbitrary",), vmem_limit_bytes=VMEM_LIMIT),
        name="dispatch",
    )(tabs["n16"], tabs["tail_start"], tabs["tail_n16"], tabs["n_used"], fx, posm, lohi, rem)


def _expert_kernel(blk_ref, nused_ref, x_ref, wg_ref, wu_ref, wd_ref, y_ref, wgu_bf, wd_bf):
    i = pl.program_id(0)

    @pl.when((i == 0) | (blk_ref[i] != blk_ref[jnp.maximum(i - 1, 0)]))
    def _():
        wgu_bf[:, :EXPERT_FF] = wg_ref[0].astype(BF16)
        wgu_bf[:, EXPERT_FF:] = wu_ref[0].astype(BF16)
        wd_bf[...] = wd_ref[0].astype(BF16)

    @pl.when(i < nused_ref[0])
    def _():
        gu = _dot(x_ref[...], wgu_bf[...])
        hmid = (_silu(gu[:, :EXPERT_FF]) * gu[:, EXPERT_FF:]).astype(BF16)
        y_ref[...] = _dot(hmid, wd_bf[...]).astype(y_ref.dtype)


def _experts(xs, blk_e, n_used, w_g, w_u, w_d, layer):
    rmax, d = xs.shape
    nblk = rmax // EBLK
    row = lambda i, blk, nu: (jnp.minimum(i, nu[0] - 1), 0)
    wsel = lambda i, blk, nu: (layer, blk[i], 0, 0)
    return pl.pallas_call(
        _expert_kernel,
        grid_spec=pltpu.PrefetchScalarGridSpec(
            num_scalar_prefetch=2,
            grid=(nblk,),
            in_specs=[
                pl.BlockSpec((EBLK, d), row),
                pl.BlockSpec((None, 1, d, EXPERT_FF), wsel),
                pl.BlockSpec((None, 1, d, EXPERT_FF), wsel),
                pl.BlockSpec((None, 1, EXPERT_FF, d), wsel),
            ],
            out_specs=pl.BlockSpec((EBLK, d), row),
            scratch_shapes=[pltpu.VMEM((d, 2 * EXPERT_FF), BF16), pltpu.VMEM((EXPERT_FF, d), BF16)],
        ),
        out_shape=jax.ShapeDtypeStruct((rmax, d), BF16),
        input_output_aliases={2: 0},
        compiler_params=pltpu.CompilerParams(dimension_semantics=("arbitrary",), vmem_limit_bytes=VMEM_LIMIT),
        name="experts",
    )(blk_e, n_used, xs, w_g, w_u, w_d)


def _combine_kernel(n16_ref, y_ref, posmt_ref, wt_ref, meta_ref, fx_ref, h_ref, mod_ref, wsgu_ref,
                    wsd_ref, gpost_ref, rem_ref, remn_ref, o_ref, buf_ref, sem):
    t = pl.program_id(0)
    slot = lax.rem(t, 2)
    buf = buf_ref.at[slot]

    @pl.when(t == 0)
    def _():
        buf_ref[...] = jnp.zeros_like(buf_ref)
        _start_chunks(rem_ref, n16_ref[t], buf, y_ref, sem.at[slot], False)

    @pl.when(t + 1 < pl.num_programs(0))
    def _():
        _start_chunks(remn_ref, n16_ref[t + 1], buf_ref.at[1 - slot], y_ref, sem.at[1 - slot], False)

    x = fx_ref[0]
    gu = _dot(x, wsgu_ref[...])
    ff = wsd_ref.shape[0]
    o_ref[0] = _dot((_silu(gu[:, :ff]) * gu[:, ff:]).astype(BF16), wsd_ref[...])

    _wait_chunks(n16_ref[t], buf, y_ref, sem.at[slot], False)

    pad = jnp.zeros((LANES - posmt_ref.shape[0], posmt_ref.shape[1]), F32)
    rank = jnp.concatenate([posmt_ref[...], pad], axis=0).astype(BF16)
    wgt = jnp.concatenate([wt_ref[...], pad], axis=0).astype(BF16)
    lo = meta_ref[0, 0:1, :]
    hi = meta_ref[0, 1:2, :]

    def unsort_chunk(rc):
        r = (lax.broadcasted_iota(I32, (RCH, 1), 0) + rc * RCH).astype(F32)
        member = (r >= lo) & (r < hi)
        base = jnp.sum(jnp.where(member, lo, 0.0), axis=1, keepdims=True)
        mb = member.astype(BF16)
        pw = jnp.where(_dot(mb, rank) == r - base, _dot(mb, wgt), 0.0).astype(BF16)
        o_ref[0] += lax.dot_general(pw, buf[rc * RCH:(rc + 1) * RCH, :], (((0,), (0,)), ((), ())),
                                    preferred_element_type=F32)

    last_rc = RL // RCH - 1
    for rc in range(last_rc):
        unsort_chunk(rc)
    pl.when(n16_ref[t] * ALIGN > last_rc * RCH)(lambda: unsort_chunk(last_rc))

    m = mod_ref[0, 0]
    o_ref[0] = h_ref[0] + m[5:6] * _rms(o_ref[0], gpost_ref[...])


def _combine(y, posm_t, wts_t, meta, fx, h2, mods, tabs, w_sgu, w_sd, g_post, n_lat):
    b, rows, d = fx.shape
    nt = rows // TM
    ntiles = b * nt
    nlt = n_lat // TM
    ne = posm_t.shape[1]
    rem = tabs["rem"]
    tile = lambda t, *_: (t // nt, t % nt, 0)
    return pl.pallas_call(
        _combine_kernel,
        grid_spec=pltpu.PrefetchScalarGridSpec(
            num_scalar_prefetch=1,
            grid=(ntiles,),
            in_specs=[
                pl.BlockSpec(memory_space=pl.ANY),
                pl.BlockSpec((posm_t.shape[0], TM), lambda t, *_: (0, t)),
                pl.BlockSpec((posm_t.shape[0], TM), lambda t, *_: (0, t)),
                pl.BlockSpec((1,) + meta.shape[1:], lambda t, *_: (t, 0, 0)),
                pl.BlockSpec((1, TM, d), tile),
                pl.BlockSpec((1, TM, d), tile),
                pl.BlockSpec((1, 1, 6, d), lambda t, *_: (t // nt, jnp.where(t % nt < nlt, 1, 0), 0, 0)),
                pl.BlockSpec(w_sgu.shape, lambda t, *_: (0, 0), pipeline_mode=pl.Buffered(1)),
                pl.BlockSpec(w_sd.shape, lambda t, *_: (0, 0), pipeline_mode=pl.Buffered(1)),
                pl.BlockSpec(g_post.shape, lambda t, *_: (0, 0), pipeline_mode=pl.Buffered(1)),
                pl.BlockSpec((1,) + rem.shape[1:], lambda t, *_: (t, 0, 0), memory_space=pltpu.SMEM),
                pl.BlockSpec((1,) + rem.shape[1:], lambda t, *_: (jnp.minimum(t + 1, ntiles - 1), 0, 0),
                             memory_space=pltpu.SMEM),
            ],
            out_specs=pl.BlockSpec((1, TM, d), tile),
            scratch_shapes=[pltpu.VMEM((2, RL, d), BF16), pltpu.SemaphoreType.DMA((2,))],
        ),
        out_shape=jax.ShapeDtypeStruct((b, rows, d), F32),
        compiler_params=pltpu.CompilerParams(dimension_semantics=("arbitrary",), vmem_limit_bytes=VMEM_LIMIT),
        name="combine",
    )(tabs["n16"], y, posm_t, wts_t, meta, fx, h2, mods, w_sgu, w_sd, g_post, rem, rem)


def _rope_tables(n_lat, n_ctx, rot_dim, lane_slots):
    rows = n_lat // GRID_W
    pos_row = jnp.repeat(jnp.arange(rows, dtype=F32), GRID_W)
    pos_col = jnp.tile(jnp.arange(GRID_W, dtype=F32), rows)
    d_axis = rot_dim // 2
    inv_freq = ROPE_BASE ** (-jnp.arange(0, d_axis, 2, dtype=F32) / d_axis)
    ar = pos_row[:, None] * inv_freq
    ac = pos_col[:, None] * inv_freq
    ang = jnp.concatenate([ar, ar, ac, ac], axis=1)
    qtr = rot_dim // 4
    first = (jnp.arange(rot_dim) // qtr) % 2 == 0
    cos, sin = jnp.cos(ang), jnp.sin(ang)
    s1 = jnp.where(first, -sin, 0.0)
    s2 = jnp.where(first, 0.0, sin)
    lc = n_lat + n_ctx
    c_t = jnp.ones((lc, LANES), F32)
    s1_t = jnp.zeros((lc, LANES), F32)
    s2_t = jnp.zeros((lc, LANES), F32)
    for lo in lane_slots:
        c_t = c_t.at[:n_lat, lo:lo + rot_dim].set(cos)
        s1_t = s1_t.at[:n_lat, lo:lo + rot_dim].set(s1)
        s2_t = s2_t.at[:n_lat, lo:lo + rot_dim].set(s2)
    return jnp.stack([c_t, s1_t, s2_t])


def _pack_layer_weights(w_in, w_uq, w_ukv):
    d = w_in.shape[0]
    kv0 = 0
    kr0 = kv0 + MLA_KV_RANK
    kd0 = kr0 + MLA_ROPE
    vd0 = kd0 + 512
    q0 = vd0 + 512
    qd0 = q0 + MLA_Q_RANK
    p0 = qd0 + 512
    g0 = p0 + 512
    kr = jnp.zeros((d, LANES), w_in.dtype).at[:, MLA_NOPE:MLA_NOPE + MLA_ROPE].set(w_in[:, kr0:kd0])
    w1 = jnp.concatenate([w_in[:, kv0:kr0], kr, w_in[:, kd0:vd0], w_in[:, vd0:q0], w_in[:, q0:qd0],
                          w_in[:, qd0:p0], w_in[:, p0:g0], w_in[:, g0:]], axis=1).astype(BF16)
    ukv = w_ukv.reshape(MLA_KV_RANK, MLA_HEADS, MLA_NOPE + MLA_V)
    uk = jnp.pad(ukv[:, :, :MLA_NOPE], ((0, 0), (0, 0), (0, LANES - MLA_NOPE))).reshape(MLA_KV_RANK, -1)
    uv = ukv[:, :, MLA_NOPE:].reshape(MLA_KV_RANK, -1)
    w_ukv_p = jnp.concatenate([uk, uv], axis=1).astype(BF16)
    uq = w_uq.reshape(MLA_Q_RANK, MLA_HEADS, MLA_NOPE + MLA_ROPE)
    w_uq_p = jnp.pad(uq, ((0, 0), (0, 0), (0, LANES - MLA_NOPE - MLA_ROPE))).reshape(MLA_Q_RANK, -1).astype(BF16)
    return w1, w_ukv_p, w_uq_p


def _routing_tables(meta, totals, rmax):
    cpad, loc, gpos = meta[:, :, 0], meta[:, :, 1], meta[:, :, 2]
    tot = totals[:, 0]
    reg = (tot + EBLK - 1) // EBLK * EBLK
    reg_end = jnp.cumsum(reg)
    reg_start = reg_end - reg
    n_used = jnp.maximum(reg_end[-1:] // EBLK, 1).astype(I32)
    blk_start = jnp.arange(rmax // EBLK, dtype=I32) * EBLK
    blk_e = jnp.minimum(jnp.sum(reg_end[None, :] <= blk_start[:, None], axis=1), N_EXPERTS - 1).astype(I32)
    ntiles, ne = loc.shape
    lohi = jnp.zeros((ntiles, 8, LANES), F32).at[:, 0, :].set(float(RL)).at[:, 0, :ne].set(loc.astype(F32))
    lohi = lohi.at[:, 1, :ne].set((loc + cpad).astype(F32))
    chunk = jnp.arange(RL // ALIGN, dtype=I32)[None, :, None]
    seg_lo = (loc // ALIGN)[:, None, :]
    seg_end = (loc + cpad) // ALIGN
    inside = (seg_lo <= chunk) & (chunk < seg_end[:, None, :])
    shift = jnp.sum(jnp.where(inside, (reg_start[None, :] + gpos - loc)[:, None, :], 0), axis=-1)
    rem = (shift + chunk[:, :, 0] * ALIGN).astype(I32)[:, None, :]
    return dict(
        n16=seg_end[:, -1].astype(I32), rem=rem,
        tail_start=(reg_start + tot).astype(I32), tail_n16=((reg - tot) // ALIGN).astype(I32),
        n_used=n_used), blk_e, lohi


def kernel(x, c, ctx, c_ctx, w_mod, b_mod, g_pre_mix, g_post_mix, g_pre_ffn, g_post_ffn, w_in, g_cq, w_uq, g_ckv, w_ukv, w_pool, pool_scale, lam_q1, lam_k1, lam_q2, lam_k2, g_subln, w_b_mla, w_b_pool, w_b_diff, w_out, w_router, b_router, w_e_gate, w_e_up, w_e_down, w_s_gate, w_s_up, w_s_down):
    b, n_lat, d = x.shape
    n_ctx = ctx.shape[1]
    depth = w_mod.shape[0]
    assert n_ctx == TM and n_lat % QROWS == 0 and n_lat % GRID_W == 0
    lc = n_lat + n_ctx
    row2 = lambda v: v.reshape(1, -1)

    rows = (b + 1 + 7) // 8 * 8
    cvec = jnp.zeros((rows, d), F32).at[:b].set(c).at[b].set(c_ctx)
    mod_all = _modulation(cvec, w_mod, b_mod).reshape(depth, rows, 6, d)

    tab_m = _rope_tables(n_lat, n_ctx, MLA_ROPE, (MLA_NOPE,))
    tab_d = _rope_tables(n_lat, n_ctx, DIFF_HALF, (0, DIFF_HALF))

    h = jnp.concatenate([x, ctx], axis=1)
    for l in range(depth):
        last = l == depth - 1
        lam_init = 0.8 - 0.6 * math.exp(-0.3 * l)
        mods = jnp.stack([jnp.broadcast_to(mod_all[l, b], (b, 6, d)), mod_all[l, :b]], axis=1)
        w1, w_ukv_p, w_uq_p = _pack_layer_weights(w_in[l], w_uq[l], w_ukv[l])

        qm, km, vm, qd, kd, vd, pool_in, gates = _project(
            h, mods, row2(g_pre_mix[l]), w1, row2(g_ckv[l]), w_ukv_p, row2(g_cq[l]), w_uq_p, tab_m, tab_d, n_lat)
        o_mla = _attention(
            functools.partial(_mla_kernel, n_lat=n_lat, has_ctx=not last), "mla_attention", qm, km, vm, (),
            [pltpu.VMEM((2, lc, 2 * MLA_V), BF16)], n_lat, not last, MLA_HEADS // 2, 2 * LANES, 2 * MLA_V, 2 * MLA_V)
        lam_p = jnp.stack([lam_q1[l], lam_k1[l], lam_q2[l], lam_k2[l]])
        o_diff = _attention(
            functools.partial(_diff_kernel, n_lat=n_lat, has_ctx=not last, lam_init=lam_init), "diff_attention",
            qd, kd, vd, (lam_p, row2(g_subln[l])), [], n_lat, not last, DIFF_HEADS, LANES, DIFF_V, DIFF_V)
        segments = ((0, n_lat),) if last else ((0, n_lat), (n_lat, n_ctx))
        o_pool = _pool_mix(pool_in, w_pool[l].astype(BF16), row2(pool_scale[l]), segments)

        nt = (n_lat if last else lc) // TM
        wb = jnp.stack([w_b_mla[l], w_b_pool[l], w_b_diff[l]]).astype(BF16)
        h2, fx, wts, posm, meta, totals = _merge(
            o_mla, o_diff, o_pool, gates, h, mods, wb, w_out[l].astype(BF16), row2(g_post_mix[l]),
            row2(g_pre_ffn[l]), w_router[l].T, b_router[l].reshape(-1, 1), n_lat, nt)
        rmax = meta.shape[0] * RL + N_EXPERTS * EBLK
        tabs, blk_e, lohi = _routing_tables(meta, totals, rmax)
        xs = _dispatch(fx, posm, lohi, tabs, rmax)
        ys = _experts(xs, blk_e, tabs["n_used"], w_e_gate, w_e_up, w_e_down, l)
        w_sgu = jnp.concatenate([w_s_gate[l], w_s_up[l]], axis=-1).astype(BF16)
        h = _combine(ys, posm, wts, lohi, fx, h2, mods, tabs, w_sgu, w_s_down[l].astype(BF16),
                     row2(g_post_ffn[l]), n_lat)
    return h
```
